```python
import jax
import jax.numpy as jnp
from jax import lax
import numpy as np

D_MODEL = 2048
BATCH = 1
SEQ = 8192
DEPTH = 2

GLA_V = D_MODEL // 2
GLA_DV = 128
GLA_HEADS = GLA_V // GLA_DV
GLA_DK = GLA_DV // 2
GLA_QK = GLA_HEADS * GLA_DK
GLA_GATE_RANK = 16
GLA_TAU = 16.0
GLA_CHUNK = 64

RWKV_WIDTH = D_MODEL - GLA_V
RWKV_HEAD = 64
RWKV_HEADS = RWKV_WIDTH // RWKV_HEAD
RWKV_W_RANK = 96
RWKV_A_RANK = 96
RWKV_G_RANK = 256
RWKV_V_RANK = 64
RWKV_GN_EPS = 64e-5

D_MIX = GLA_V + RWKV_WIDTH
D_FF = 4 * D_MODEL
RMS_EPS = 1e-6

GLA_SPLITS = (GLA_QK, GLA_QK, GLA_V, GLA_V, GLA_GATE_RANK)
RWKV_SPLITS = (RWKV_WIDTH, RWKV_WIDTH, RWKV_WIDTH, RWKV_W_RANK, RWKV_A_RANK, RWKV_G_RANK)
GLA_COLS = sum(GLA_SPLITS)
RWKV_COLS = sum(RWKV_SPLITS)
N_IN = GLA_COLS + RWKV_COLS

kernel_name = 'hymba_gla_rwkv7_hybrid'


def rmsnorm(x, g):
    xf = x.astype(jnp.float32)
    y = xf * lax.rsqrt(jnp.mean(xf * xf, axis=-1, keepdims=True) + RMS_EPS)
    return (y * g.astype(jnp.float32)).astype(x.dtype)


def token_shift(z):
    return jnp.pad(z, ((0, 0), (1, 0), (0, 0)))[:, :-1]


def lerp_shift(z, mu):
    return z + mu * (token_shift(z) - z)


def split_cols(z, sizes):
    return jnp.split(z, np.cumsum(sizes)[:-1].tolist(), axis=-1)


def gla_mixer(q, k, v, g, a_low, w_a_up, b_a, norm_w):
    f32 = jnp.float32
    B, T, _ = q.shape
    H, DK, DV, C = GLA_HEADS, GLA_DK, GLA_DV, GLA_CHUNK
    n_chunks = T // C
    log_a = jax.nn.log_sigmoid(a_low.astype(f32) @ w_a_up.astype(f32) + b_a.astype(f32)) / GLA_TAU

    def to_chunks(z, d):
        return z.astype(f32).reshape(B, n_chunks, C, H, d).transpose(1, 0, 3, 2, 4)

    qc = to_chunks(q, DK) * (DK ** -0.5)
    kc = to_chunks(k, DK)
    vc = to_chunks(v, DV)
    lac = to_chunks(log_a, DK)
    causal = jnp.tril(jnp.ones((C, C), dtype=bool))[None, None, :, :, None]

    def chunk_step(S, inp):
        qi, ki, vi, lai = inp
        b = jnp.cumsum(lai, axis=2)
        o_inter = jnp.einsum('bhck,bhkv->bhcv', qi * jnp.exp(b), S)
        rel = b[:, :, :, None, :] - b[:, :, None, :, :]
        decay = jnp.where(causal, jnp.exp(jnp.minimum(rel, 0.0)), 0.0)
        scores = jnp.einsum('bhik,bhjk,bhijk->bhij', qi, ki, decay)
        o_intra = jnp.einsum('bhij,bhjv->bhiv', scores, vi)
        b_last = b[:, :, -1:, :]
        S = S * jnp.exp(b_last[:, :, 0, :, None]) + jnp.einsum('bhck,bhcv->bhkv', ki * jnp.exp(b_last - b), vi)
        return S, o_inter + o_intra

    S0 = jnp.zeros((B, H, DK, DV), f32)
    _, o = lax.scan(chunk_step, S0, (qc, kc, vc, lac))
    o = o.transpose(1, 0, 3, 2, 4).reshape(B, T, H, DV)
    o = o * lax.rsqrt(jnp.mean(o * o, axis=-1, keepdims=True) + RMS_EPS)
    o = o.reshape(B, T, GLA_V) * norm_w.astype(f32)
    return (o * jax.nn.silu(g.astype(f32))).astype(q.dtype)


def rwkv7_mixer(r, k, v, w_low, a_low, g_low, w0, w_up, a0, a_up, g_up, k_k, k_a, r_k, gn_w, gn_b):
    f32 = jnp.float32
    B, T, _ = r.shape
    H, N = RWKV_HEADS, RWKV_HEAD
    out_dtype = r.dtype
    r, k, v, w_low, a_low, g_low = (z.astype(f32) for z in (r, k, v, w_low, a_low, g_low))
    w = -jax.nn.softplus(-(w0.astype(f32) + jnp.tanh(w_low) @ w_up.astype(f32))) - 0.5
    decay = jnp.exp(-jnp.exp(w))
    a = jax.nn.sigmoid(a0.astype(f32) + a_low @ a_up.astype(f32))
    gate = jax.nn.sigmoid(g_low) @ g_up.astype(f32)
    kk = (k * k_k.astype(f32)).reshape(B, T, H, N)
    kk = kk * lax.rsqrt(jnp.maximum(jnp.sum(kk * kk, axis=-1, keepdims=True), 1e-24))
    k = k * (1.0 + (a - 1.0) * k_a.astype(f32))
    rh = r.reshape(B, T, H, N)
    kh = k.reshape(B, T, H, N)
    vh = v.reshape(B, T, H, N)
    wh = decay.reshape(B, T, H, N)
    ah = a.reshape(B, T, H, N)

    def tm(z):
        return jnp.moveaxis(z, 1, 0)

    def step(S, inp):
        r_t, w_t, k_t, v_t, a_t, b_t = inp
        sa = jnp.einsum('bhvk,bhk->bhv', S, a_t)
        S = S * w_t[:, :, None, :] + sa[..., None] * b_t[:, :, None, :] + v_t[..., None] * k_t[:, :, None, :]
        return S, jnp.einsum('bhvk,bhk->bhv', S, r_t)

    S0 = jnp.zeros((B, H, N, N), f32)
    _, y = lax.scan(step, S0, (tm(rh), tm(wh), tm(kh), tm(vh), tm(-kk), tm(kk * ah)))
    y = jnp.moveaxis(y, 0, 1)
    mu = jnp.mean(y, axis=-1, keepdims=True)
    var = jnp.mean(jnp.square(y - mu), axis=-1, keepdims=True)
    y = ((y - mu) * lax.rsqrt(var + RWKV_GN_EPS)).reshape(B, T, RWKV_WIDTH)
    y = y * gn_w.astype(f32) + gn_b.astype(f32)
    bonus = jnp.sum(rh * kh * r_k.astype(f32), axis=-1, keepdims=True) * vh
    y = y + bonus.reshape(B, T, RWKV_WIDTH)
    return (y * gate).astype(out_dtype)


def setup_inputs(seed: int = 0) -> dict:
    key = jax.random.key(seed)
    ks = list(jax.random.split(key, 40))
    it = iter(ks)

    def nrm(shape, scale):
        return scale * jax.random.normal(next(it), shape, jnp.float32)

    def unif(shape):
        return jax.random.uniform(next(it), shape, jnp.float32)

    D, L, RW = D_MODEL, DEPTH, RWKV_WIDTH
    ch = jnp.linspace(0.0, 1.0, RW, dtype=jnp.float32)
    return {
        'x': nrm((BATCH, SEQ, D), 1.0),
        'c': nrm((BATCH, D), 1.0),
        'w_ada': nrm((L, D, 6 * D), 0.5 * D ** -0.5),
        'b_ada': nrm((L, 6 * D), 0.02),
        'g_pre_mix': 1.0 + nrm((L, D), 0.02),
        'g_post_mix': 1.0 + nrm((L, D), 0.02),
        'g_pre_ffn': 1.0 + nrm((L, D), 0.02),
        'g_post_ffn': 1.0 + nrm((L, D), 0.02),
        'w_in': nrm((L, D, N_IN), D ** -0.5),
        'gla_w_a_up': nrm((L, GLA_GATE_RANK, GLA_QK), GLA_GATE_RANK ** -0.5),
        'gla_b_a': 1.0 + nrm((L, GLA_QK), 0.5),
        'gla_norm_w': 1.0 + nrm((L, GLA_V), 0.02),
        'rwkv_mu': unif((L, RWKV_COLS)),
        'rwkv_w0': -6.5 + 5.0 * ch ** 0.85 + nrm((L, RW), 0.1),
        'rwkv_w_up': nrm((L, RWKV_W_RANK, RW), 0.5 * RWKV_W_RANK ** -0.5),
        'rwkv_a0': nrm((L, RW), 0.1),
        'rwkv_a_up': nrm((L, RWKV_A_RANK, RW), RWKV_A_RANK ** -0.5),
        'rwkv_g_up': nrm((L, RWKV_G_RANK, RW), RWKV_G_RANK ** -0.5),
        'rwkv_k_k': 0.85 + nrm((L, RW), 0.02),
        'rwkv_k_a': 1.0 + nrm((L, RW), 0.02),
        'rwkv_r_k': nrm((L, RWKV_HEADS, RWKV_HEAD), 0.1),
        'rwkv_gn_w': 1.0 + nrm((L, RW), 0.02),
        'rwkv_gn_b': nrm((L, RW), 0.02),
        'vres_w_down': nrm((L - 1, D, RWKV_V_RANK), D ** -0.5),
        'vres_mu': unif((L - 1, RWKV_V_RANK)),
        'vres_up': nrm((L - 1, RWKV_V_RANK, RW), RWKV_V_RANK ** -0.5),
        'vres_v0': 1.0 + nrm((L - 1, RW), 0.1),
        'w_out': nrm((L, D_MIX, D), D_MIX ** -0.5),
        'w_ff1': nrm((L, D, D_FF), D ** -0.5),
        'w_ff2': nrm((L, D_FF, D), D_FF ** -0.5),
    }


def reference(x, c, w_ada, b_ada, g_pre_mix, g_post_mix, g_pre_ffn, g_post_ffn, w_in,
              gla_w_a_up, gla_b_a, gla_norm_w, rwkv_mu, rwkv_w0, rwkv_w_up, rwkv_a0, rwkv_a_up,
              rwkv_g_up, rwkv_k_k, rwkv_k_a, rwkv_r_k, rwkv_gn_w, rwkv_gn_b, vres_w_down, vres_mu,
              vres_up, vres_v0, w_out, w_ff1, w_ff2):
    cond = jax.nn.silu(c)
    v_first = None
    for i in range(DEPTH):
        mod = cond @ w_ada[i] + b_ada[i]
        sh1, sc1, gt1, sh2, sc2, gt2 = (m[:, None, :] for m in jnp.split(mod, 6, axis=-1))

        h = rmsnorm(x, g_pre_mix[i]) * (1.0 + sc1) + sh1
        z = h @ w_in[i]
        z_gla = z[..., :GLA_COLS]
        z_rwkv = lerp_shift(z[..., GLA_COLS:], rwkv_mu[i])
        gq, gk, gv, gg, ga = split_cols(z_gla, GLA_SPLITS)
        rr, rk, rv, rw, ra, rg = split_cols(z_rwkv, RWKV_SPLITS)
        if i == 0:
            v_first = rv
        else:
            j = i - 1
            vl = lerp_shift(h @ vres_w_down[j], vres_mu[j])
            rv = rv + (v_first - rv) * jax.nn.sigmoid(vres_v0[j] + vl @ vres_up[j])
        o_gla = gla_mixer(gq, gk, gv, gg, ga, gla_w_a_up[i], gla_b_a[i], gla_norm_w[i])
        o_rwkv = rwkv7_mixer(rr, rk, rv, rw, ra, rg, rwkv_w0[i], rwkv_w_up[i], rwkv_a0[i], rwkv_a_up[i],
                             rwkv_g_up[i], rwkv_k_k[i], rwkv_k_a[i], rwkv_r_k[i], rwkv_gn_w[i], rwkv_gn_b[i])
        y = jnp.concatenate([o_gla, o_rwkv], axis=-1) @ w_out[i]
        x = x + gt1 * rmsnorm(y, g_post_mix[i])

        h = rmsnorm(x, g_pre_ffn[i]) * (1.0 + sc2) + sh2
        y = jnp.square(jax.nn.relu(h @ w_ff1[i])) @ w_ff2[i]
        x = x + gt2 * rmsnorm(y, g_post_ffn[i])
    return x
```

```python
import functools

import numpy as np
import jax
import jax.numpy as jnp
from jax import lax
from jax.experimental import pallas as pl
from jax.experimental.pallas import tpu as pltpu

F32 = jnp.float32
BF16 = jnp.bfloat16

D_MODEL = 2048
DEPTH = 2
GLA_V = 1024
GLA_DV = 128
GLA_HEADS = 8
GLA_DK = 64
GLA_QK = 512
GLA_GATE_RANK = 16
GLA_TAU = 16.0
RWKV_WIDTH = 1024
RWKV_HEAD = 64
RWKV_HEADS = 16
RWKV_W_RANK = 96
RWKV_A_RANK = 96
RWKV_G_RANK = 256
RWKV_V_RANK = 64
RWKV_GN_EPS = 64e-5
D_FF = 4 * D_MODEL
RMS_EPS = 1e-6

LANES = 128
CHUNK = 64
STACK = 2 * CHUNK
VMEM_LIMIT_BYTES = 56 * 1024 * 1024

C_GQ = 0
C_GK = 512
C_GV = 1024
C_GG = 2048
C_GA = 3072
C_VRES = 3200
C_RR = 3328
C_RK = 4352
C_RV = 5376
C_RG = 6400
C_RW = 6656
C_RA = 6784
N_PACK = 6912


def _dot(a, b):
    return jnp.dot(a, b, preferred_element_type=F32)


def _dot_nt(a, b):
    return lax.dot_general(a, b, (((1,), (1,)), ((), ())), preferred_element_type=F32)


def _dot_f32(a, b):
    return jnp.dot(a, b, preferred_element_type=F32, precision=lax.Precision.HIGHEST)


def _dot_exact_lhs(p, x):
    hi = x.astype(BF16)
    r1 = x - hi.astype(F32)
    mid = r1.astype(BF16)
    lo = (r1 - mid.astype(F32)).astype(BF16)
    return _dot(p, hi) + _dot(p, mid) + _dot(p, lo)


def _dot_exact_rhs(x, p):
    hi = x.astype(BF16)
    r1 = x - hi.astype(F32)
    mid = r1.astype(BF16)
    lo = (r1 - mid.astype(F32)).astype(BF16)
    return _dot(hi, p) + _dot(mid, p) + _dot(lo, p)


def _log_sigmoid(x):
    return jnp.minimum(x, 0.0) - jnp.log1p(jnp.exp(-jnp.abs(x)))


def _sigmoid(x):
    return 1.0 / (1.0 + jnp.exp(-x))


def _stack_heads(x):
    lane = lax.broadcasted_iota(jnp.int32, x.shape, 1)
    first = lane < (LANES // 2)
    return jnp.concatenate([jnp.where(first, x, 0.0), jnp.where(first, 0.0, x)], axis=0)


ADA_TN = 1024


def _ada_kernel(c_ref, w_ref, b_ref, o_ref):
    c = c_ref[...]
    cond = c * _sigmoid(c)
    o_ref[0] = jnp.sum(w_ref[0] * cond, axis=0, keepdims=True) + b_ref[0]


def _ada_call(c_col, w_ada, b_ada3):
    n_layers, d, n = w_ada.shape
    return pl.pallas_call(
        _ada_kernel,
        out_shape=jax.ShapeDtypeStruct((n_layers, 1, n), F32),
        grid=(n_layers, n // ADA_TN),
        in_specs=[
            pl.BlockSpec((d, 1), lambda l, j: (0, 0)),
            pl.BlockSpec((1, d, ADA_TN), lambda l, j: (l, 0, j)),
            pl.BlockSpec((1, 1, ADA_TN), lambda l, j: (l, 0, j)),
        ],
        out_specs=pl.BlockSpec((1, 1, ADA_TN), lambda l, j: (l, 0, j)),
        compiler_params=pltpu.CompilerParams(
            dimension_semantics=("arbitrary", "arbitrary"), vmem_limit_bytes=VMEM_LIMIT_BYTES),
        name="ada_mod",
    )(c_col, w_ada, b_ada3)


INP_TM = 512
INP_TN = 768


def _inproj_kernel(x_ref, g_ref, sh_ref, sc_ref, w_ref, mu_ref, z_ref, h_scr, carry_scr):
    i = pl.program_id(0)
    j = pl.program_id(1)

    @pl.when(j == 0)
    def _():
        x = x_ref[...]
        ms = jnp.mean(x * x, axis=-1, keepdims=True)
        y = x * lax.rsqrt(ms + RMS_EPS) * g_ref[...]
        h_scr[...] = (y * (1.0 + sc_ref[0]) + sh_ref[0]).astype(BF16)

    @pl.when(i == 0)
    def _():
        carry_scr[j] = jnp.zeros(carry_scr.shape[1:], F32)

    z = _dot(h_scr[...], w_ref[...])
    prev = carry_scr[j]
    row = lax.broadcasted_iota(jnp.int32, z.shape, 0)
    shifted = jnp.where(row == 0, prev, pltpu.roll(z, 1, 0))
    carry_scr[j] = z[z.shape[0] - 1:, :]
    z_ref[...] = z + mu_ref[...] * (shifted - z)


def _inproj_call(x2, g_row, mod, layer, w_pack, mu_pack):
    t, d = x2.shape
    n = w_pack.shape[1]
    nj = n // INP_TN
    return pl.pallas_call(
        _inproj_kernel,
        out_shape=jax.ShapeDtypeStruct((t, n), F32),
        grid=(t // INP_TM, nj),
        in_specs=[
            pl.BlockSpec((INP_TM, d), lambda i, j: (i, 0)),
            pl.BlockSpec((1, d), lambda i, j: (0, 0)),
            pl.BlockSpec((1, 1, d), lambda i, j: (layer, 0, 0)),
            pl.BlockSpec((1, 1, d), lambda i, j: (layer, 0, 1)),
            pl.BlockSpec((d, INP_TN), lambda i, j: (0, j)),
            pl.BlockSpec((1, INP_TN), lambda i, j: (0, j)),
        ],
        out_specs=pl.BlockSpec((INP_TM, INP_TN), lambda i, j: (i, j)),
        scratch_shapes=[pltpu.VMEM((INP_TM, d), BF16), pltpu.VMEM((nj, 1, INP_TN), F32)],
        compiler_params=pltpu.CompilerParams(
            dimension_semantics=("arbitrary", "arbitrary"), vmem_limit_bytes=VMEM_LIMIT_BYTES),
        name="inproj",
    )(x2, g_row, mod, mod, w_pack, mu_pack)


GLA_NCH = 2
GLA_LEVELS = (32, 16, 8, 4, 2, 1)


def _gla_exponent_matrix():
    c = CHUNK
    p = np.zeros((2 * c + len(GLA_LEVELS) * c, c), np.float32)
    for i in range(c):
        p[i, : i + 1] = 1.0
        p[c + i, i + 1:] = 1.0
    for li, s in enumerate(GLA_LEVELS):
        base = 2 * c + li * c
        for i in range(c):
            m = (i // (2 * s)) * (2 * s) + s
            if i & s:
                p[base + i, m + 1: i + 1] = 1.0
            else:
                p[base + i, i + 1: m + 1] = 1.0
    return p


def _gla_kernel(q_ref, k_ref, v_ref, g_ref, al_ref, wup_ref, ba_ref, nw_ref, p_ref, o_ref, st_scr):
    @pl.when(pl.program_id(1) == 0)
    def _():
        st_scr[...] = jnp.zeros(st_scr.shape, F32)

    c = CHUNK
    row64 = lax.broadcasted_iota(jnp.int32, (c, LANES), 0)
    rr = lax.broadcasted_iota(jnp.int32, (STACK, STACK), 0)
    cc = lax.broadcasted_iota(jnp.int32, (STACK, STACK), 1)
    nw = nw_ref[...]

    for ci in range(GLA_NCH):
        sl = pl.ds(ci * c, c)
        q = q_ref[sl, :] * (GLA_DK ** -0.5)
        k = k_ref[sl, :]
        v = v_ref[sl, :]
        g = g_ref[sl, :]
        x = _dot(al_ref[sl, :].astype(BF16), wup_ref[...]) + ba_ref[...]
        la = _log_sigmoid(x) * (1.0 / GLA_TAU)
        e_all = _dot_exact_lhs(p_ref[...], la)
        b = e_all[0:c]
        b_rest = e_all[c:2 * c]
        b_last = b[c - 1:c, :]

        v_cat = jnp.concatenate([v[:, :LANES], v[:, LANES:]], axis=0)
        v_cat_b = v_cat.astype(BF16)

        q_st = _stack_heads(q)
        k_st = _stack_heads(k)
        scores = jnp.where(rr == cc, _dot_nt(q_st.astype(BF16), k_st.astype(BF16)), 0.0)
        for li, s in enumerate(GLA_LEVELS):
            e = jnp.exp(e_all[(2 + li) * c:(3 + li) * c])
            second = (row64 & s) != 0
            qd = _stack_heads(jnp.where(second, q * e, 0.0)).astype(BF16)
            kd = _stack_heads(jnp.where(second, 0.0, k * e)).astype(BF16)
            shift = int(np.log2(2 * s))
            same = (rr >> shift) == (cc >> shift)
            scores = scores + jnp.where(same, _dot_nt(qd, kd), 0.0)
        o_intra = _dot(scores.astype(BF16), v_cat_b)

        st = st_scr[...]
        qe = _stack_heads(q * jnp.exp(b)).astype(BF16)
        o_inter = _dot_nt(qe, st.astype(BF16))
        ke = _stack_heads(k * jnp.exp(b_rest)).astype(BF16)
        st_scr[...] = st * jnp.exp(b_last) + _dot(v_cat.T.astype(BF16), ke)

        o = o_inter + o_intra
        o = o * lax.rsqrt(jnp.mean(o * o, axis=-1, keepdims=True) + RMS_EPS)
        gs = g * _sigmoid(g)
        o0 = o[:c] * nw[:, :LANES] * gs[:, :LANES]
        o1 = o[c:] * nw[:, LANES:] * gs[:, LANES:]
        o_ref[sl, :] = jnp.concatenate([o0, o1], axis=1).astype(o_ref.dtype)


def _gla_call(z, wup_pad, ba_row, nw_row, p_mat):
    t = z.shape[0]
    tb = CHUNK * GLA_NCH
    pairs = GLA_HEADS // 2
    blk = lambda w, off: pl.BlockSpec((tb, w), lambda p, c: (c, off // w + p))
    return pl.pallas_call(
        _gla_kernel,
        out_shape=jax.ShapeDtypeStruct((t, GLA_V), BF16),
        grid=(pairs, t // tb),
        in_specs=[
            blk(LANES, C_GQ), blk(LANES, C_GK), blk(2 * LANES, C_GV), blk(2 * LANES, C_GG),
            pl.BlockSpec((tb, LANES), lambda p, c: (c, C_GA // LANES)),
            pl.BlockSpec((LANES, LANES), lambda p, c: (0, p)),
            pl.BlockSpec((1, LANES), lambda p, c: (0, p)),
            pl.BlockSpec((1, 2 * LANES), lambda p, c: (0, p)),
            pl.BlockSpec(p_mat.shape, lambda p, c: (0, 0)),
        ],
        out_specs=pl.BlockSpec((tb, 2 * LANES), lambda p, c: (c, p)),
        scratch_shapes=[pltpu.VMEM((GLA_DV, LANES), F32)],
        compiler_params=pltpu.CompilerParams(
            dimension_semantics=("arbitrary", "arbitrary"), vmem_limit_bytes=VMEM_LIMIT_BYTES),
        name="gla_mixer",
    )(z, z, z, z, z, wup_pad, ba_row, nw_row, p_mat)


RWKV_NCH = 2
RWKV_INV_LEVELS = (2, 4, 8, 16, 32)


def _rwkv_kernel(*refs, has_vres):
    if has_vres:
        (r_ref, k_ref, v_ref, wl_ref, al_ref, gl_ref, vl_ref, vf_ref, vup_ref, v0_ref,
         w0_ref, wup_ref, a0_ref, aup_ref, gup_ref, kk_ref, ka_ref, rk_ref, gnw_ref, gnb_ref,
         tri_ref, ones_ref, o_ref, st_scr) = refs
    else:
        (r_ref, k_ref, v_ref, wl_ref, al_ref, gl_ref,
         w0_ref, wup_ref, a0_ref, aup_ref, gup_ref, kk_ref, ka_ref, rk_ref, gnw_ref, gnb_ref,
         tri_ref, ones_ref, o_ref, st_scr) = refs

    @pl.when(pl.program_id(1) == 0)
    def _():
        st_scr[...] = jnp.zeros(st_scr.shape, F32)

    c = CHUNK
    rr = lax.broadcasted_iota(jnp.int32, (STACK, STACK), 0)
    cc = lax.broadcasted_iota(jnp.int32, (STACK, STACK), 1)
    strict = rr > cc
    incl = rr >= cc
    eye = (rr == cc).astype(F32)
    ones_bd = ones_ref[...]
    inv_n = 1.0 / RWKV_HEAD

    def seg_sum(x):
        return _dot_exact_rhs(x, ones_bd)

    for ci in range(RWKV_NCH):
        sl = pl.ds(ci * c, c)
        r = r_ref[sl, :]
        k = k_ref[sl, :]
        v = v_ref[sl, :]
        w_pre = w0_ref[...] + _dot(jnp.tanh(wl_ref[sl, :]).astype(BF16), wup_ref[...])
        lw = -jnp.exp(_log_sigmoid(w_pre) - 0.5)
        alr = _sigmoid(a0_ref[...] + _dot(al_ref[sl, :].astype(BF16), aup_ref[...]))
        gate = _dot(_sigmoid(gl_ref[sl, :]).astype(BF16), gup_ref[...])
        if has_vres:
            mix = _sigmoid(v0_ref[...] + _dot(vl_ref[sl, :].astype(BF16), vup_ref[...]))
            v = v + (vf_ref[sl, :] - v) * mix
        kk = k * kk_ref[...]
        kk = kk * lax.rsqrt(jnp.maximum(seg_sum(kk * kk), 1e-24))
        k2 = k * (1.0 + (alr - 1.0) * ka_ref[...])
        a = -kk
        b = kk * alr

        cl = _dot_exact_lhs(tri_ref[...], lw)
        cl_last = cl[c - 1:c, :]
        e_neg = jnp.exp(-cl)
        e_end = jnp.exp(cl_last - cl)
        at_s = _stack_heads(a * jnp.exp(cl - lw))
        rt_s = _stack_heads(r * jnp.exp(cl))
        bt_s = _stack_heads(b * e_neg)
        kt_s = _stack_heads(k2 * e_neg)
        bh_s = _stack_heads(b * e_end).astype(BF16)
        kh_s = _stack_heads(k2 * e_end).astype(BF16)
        v_m = _stack_heads(v)
        v_mb = v_m.astype(BF16)

        lhs = jnp.concatenate([at_s, rt_s], axis=0).astype(BF16)
        rhs = jnp.concatenate([bt_s, kt_s], axis=0).astype(BF16)
        aa = _dot_nt(lhs, rhs)
        a_ab = jnp.where(strict, aa[:STACK, :STACK], 0.0)
        a_ak = jnp.where(strict, aa[:STACK, STACK:], 0.0)
        p_rb = jnp.where(incl, aa[STACK:, :STACK], 0.0)
        p_rk = jnp.where(incl, aa[STACK:, STACK:], 0.0)

        tinv = eye + jnp.where(((rr >> 1) == (cc >> 1)), a_ab, 0.0)
        for s in RWKV_INV_LEVELS:
            shift = int(np.log2(2 * s))
            off = ((rr >> shift) == (cc >> shift)) & ((rr & s) != 0) & ((cc & s) == 0)
            a_off = jnp.where(off, a_ab, 0.0)
            tinv = tinv + _dot_f32(_dot_f32(tinv, a_off), tinv)

        wt = _dot_f32(tinv, at_s)
        u0 = _dot_f32(tinv, _dot(a_ak.astype(BF16), v_mb))
        p_rb_b = p_rb.astype(BF16)
        qh = rt_s + _dot(p_rb_b, wt.astype(BF16))
        y0 = _dot(p_rb_b, u0.astype(BF16)) + _dot(p_rk.astype(BF16), v_mb)
        gmat = _dot(wt.T.astype(BF16), bh_s)
        n0t = _dot(u0.T.astype(BF16), bh_s) + _dot(v_m.T.astype(BF16), kh_s)

        st = st_scr[...]
        ym = _dot_nt(qh.astype(BF16), st.astype(BF16)) + y0
        st_scr[...] = st * jnp.exp(cl_last) + _dot_f32(st, gmat) + n0t

        y = ym[:c] + ym[c:]
        mean = seg_sum(y) * inv_n
        yc = y - mean
        var = seg_sum(yc * yc) * inv_n
        yn = yc * lax.rsqrt(var + RWKV_GN_EPS) * gnw_ref[...] + gnb_ref[...]
        bonus = seg_sum(r * k2 * rk_ref[...]) * v
        o_ref[sl, :] = ((yn + bonus) * gate).astype(o_ref.dtype)


def _rwkv_call(z, z_first, vec, mats, tri, ones_bd, has_vres):
    t = z.shape[0]
    tb = CHUNK * RWKV_NCH
    pairs = RWKV_HEADS // 2
    col = lambda off: pl.BlockSpec((tb, LANES), lambda p, c: (c, off // LANES + p))
    fixed = lambda off, w: pl.BlockSpec((tb, w), lambda p, c: (c, off // w))
    rowv = pl.BlockSpec((1, LANES), lambda p, c: (0, p))
    upm = lambda rows: pl.BlockSpec((rows, LANES), lambda p, c: (0, p))
    const = lambda a: pl.BlockSpec(a.shape, lambda p, c: (0, 0))

    in_specs = [col(C_RR), col(C_RK), col(C_RV), fixed(C_RW, LANES), fixed(C_RA, LANES), fixed(C_RG, 2 * LANES)]
    args = [z, z, z, z, z, z]
    if has_vres:
        in_specs += [fixed(C_VRES, LANES), col(C_RV), upm(LANES), rowv]
        args += [z, z_first, mats["v_up"], vec["v0"]]
    in_specs += [rowv, upm(LANES), rowv, upm(LANES), upm(2 * LANES), rowv, rowv, rowv, rowv, rowv,
                 const(tri), const(ones_bd)]
    args += [vec["w0"], mats["w_up"], vec["a0"], mats["a_up"], mats["g_up"], vec["k_k"], vec["k_a"],
             vec["r_k"], vec["gn_w"], vec["gn_b"], tri, ones_bd]
    return pl.pallas_call(
        functools.partial(_rwkv_kernel, has_vres=has_vres),
        out_shape=jax.ShapeDtypeStruct((t, RWKV_WIDTH), BF16),
        grid=(pairs, t // tb),
        in_specs=in_specs,
        out_specs=pl.BlockSpec((tb, LANES), lambda p, c: (c, p)),
        scratch_shapes=[pltpu.VMEM((STACK, STACK), F32)],
        compiler_params=pltpu.CompilerParams(
            dimension_semantics=("arbitrary", "arbitrary"), vmem_limit_bytes=VMEM_LIMIT_BYTES),
        name="rwkv7_mixer",
    )(*args)


OUT_TM = 256


def _outproj_kernel(x_ref, oa_ref, ob_ref, wa_ref, wb_ref, g_ref, gt_ref, o_ref):
    y = _dot(oa_ref[...], wa_ref[...]) + _dot(ob_ref[...], wb_ref[...])
    ms = jnp.mean(y * y, axis=-1, keepdims=True)
    o_ref[...] = x_ref[...] + gt_ref[0] * (y * lax.rsqrt(ms + RMS_EPS) * g_ref[...])


def _outproj_call(x2, o_gla, o_rwkv, w_a, w_b, g_row, mod, layer):
    t, d = x2.shape
    return pl.pallas_call(
        _outproj_kernel,
        out_shape=jax.ShapeDtypeStruct((t, d), F32),
        grid=(t // OUT_TM,),
        in_specs=[
            pl.BlockSpec((OUT_TM, d), lambda i: (i, 0)),
            pl.BlockSpec((OUT_TM, GLA_V), lambda i: (i, 0)),
            pl.BlockSpec((OUT_TM, RWKV_WIDTH), lambda i: (i, 0)),
            pl.BlockSpec((GLA_V, d), lambda i: (0, 0)),
            pl.BlockSpec((RWKV_WIDTH, d), lambda i: (0, 0)),
            pl.BlockSpec((1, d), lambda i: (0, 0)),
            pl.BlockSpec((1, 1, d), lambda i: (layer, 0, 2)),
        ],
        out_specs=pl.BlockSpec((OUT_TM, d), lambda i: (i, 0)),
        compiler_params=pltpu.CompilerParams(
            dimension_semantics=("arbitrary",), vmem_limit_bytes=VMEM_LIMIT_BYTES),
        name="outproj",
    )(x2, o_gla, o_rwkv, w_a, w_b, g_row, mod)


FFN_TM = 512
FFN_TF = 1024


def _ffn_kernel(x_ref, gpre_ref, sh_ref, sc_ref, w1_ref, w2_ref, gpost_ref, gt_ref, o_ref, h_scr, acc_scr):
    f = pl.program_id(1)

    @pl.when(f == 0)
    def _():
        x = x_ref[...]
        ms = jnp.mean(x * x, axis=-1, keepdims=True)
        y = x * lax.rsqrt(ms + RMS_EPS) * gpre_ref[...]
        h_scr[...] = (y * (1.0 + sc_ref[0]) + sh_ref[0]).astype(BF16)
        acc_scr[...] = jnp.zeros(acc_scr.shape, F32)

    u = jnp.maximum(_dot(h_scr[...], w1_ref[...]), 0.0)
    acc_scr[...] += _dot((u * u).astype(BF16), w2_ref[...])

    @pl.when(f == pl.num_programs(1) - 1)
    def _():
        y = acc_scr[...]
        ms = jnp.mean(y * y, axis=-1, keepdims=True)
        o_ref[...] = x_ref[...] + gt_ref[0] * (y * lax.rsqrt(ms + RMS_EPS) * gpost_ref[...])


def _ffn_call(x2, gpre_row, gpost_row, mod, layer, w1, w2):
    t, d = x2.shape
    dff = w1.shape[1]
    return pl.pallas_call(
        _ffn_kernel,
        out_shape=jax.ShapeDtypeStruct((t, d), F32),
        grid=(t // FFN_TM, dff // FFN_TF),
        in_specs=[
            pl.BlockSpec((FFN_TM, d), lambda i, f: (i, 0)),
            pl.BlockSpec((1, d), lambda i, f: (0, 0)),
            pl.BlockSpec((1, 1, d), lambda i, f: (layer, 0, 3)),
            pl.BlockSpec((1, 1, d), lambda i, f: (layer, 0, 4)),
            pl.BlockSpec((d, FFN_TF), lambda i, f: (0, f)),
            pl.BlockSpec((FFN_TF, d), lambda i, f: (f, 0)),
            pl.BlockSpec((1, d), lambda i, f: (0, 0)),
            pl.BlockSpec((1, 1, d), lambda i, f: (layer, 0, 5)),
        ],
        out_specs=pl.BlockSpec((FFN_TM, d), lambda i, f: (i, 0)),
        scratch_shapes=[pltpu.VMEM((FFN_TM, d), BF16), pltpu.VMEM((FFN_TM, d), F32)],
        compiler_params=pltpu.CompilerParams(
            dimension_semantics=("arbitrary", "arbitrary"), vmem_limit_bytes=VMEM_LIMIT_BYTES),
        name="ffn",
    )(x2, gpre_row, mod, mod, w1, w2, gpost_row, mod)


def _pad_cols(w, width):
    return jnp.pad(w, ((0, 0), (0, width - w.shape[1])))


def _pad_rows(w, rows):
    return jnp.pad(w, ((0, rows - w.shape[0]), (0, 0)))


def _pack_inproj(w_in, mu_rwkv, vres_w_down, vres_mu):
    d = w_in.shape[0]
    gq, gk, gv, gg, ga, rr, rk, rv, rw, ra, rg = jnp.split(
        w_in, np.cumsum([512, 512, 1024, 1024, 16, 1024, 1024, 1024, 96, 96]).tolist(), axis=1)
    mr, mk, mv, mw, ma, mg = jnp.split(mu_rwkv[None, :], np.cumsum([1024, 1024, 1024, 96, 96]).tolist(), axis=1)
    if vres_w_down is None:
        vres_w_down = jnp.zeros((d, RWKV_V_RANK), w_in.dtype)
        vres_mu = jnp.zeros((RWKV_V_RANK,), w_in.dtype)
    w_pack = jnp.concatenate(
        [gq, gk, gv, gg, _pad_cols(ga, LANES), _pad_cols(vres_w_down, LANES),
         rr, rk, rv, rg, _pad_cols(rw, LANES), _pad_cols(ra, LANES)], axis=1).astype(BF16)
    mu_pack = jnp.concatenate(
        [jnp.zeros((1, C_VRES), F32), _pad_cols(vres_mu[None, :], LANES),
         mr, mk, mv, mg, _pad_cols(mw, LANES), _pad_cols(ma, LANES)], axis=1)
    return w_pack, mu_pack


def kernel(x, c, w_ada, b_ada, g_pre_mix, g_post_mix, g_pre_ffn, g_post_ffn, w_in, gla_w_a_up, gla_b_a, gla_norm_w, rwkv_mu, rwkv_w0, rwkv_w_up, rwkv_a0, rwkv_a_up, rwkv_g_up, rwkv_k_k, rwkv_k_a, rwkv_r_k, rwkv_gn_w, rwkv_gn_b, vres_w_down, vres_mu, vres_up, vres_v0, w_out, w_ff1, w_ff2):
    bsz, t, d = x.shape
    assert bsz == 1 and d == D_MODEL and t % (CHUNK * max(GLA_NCH, RWKV_NCH)) == 0 and t % FFN_TM == 0
    n_layers = w_ada.shape[0]

    mod = _ada_call(c.reshape(d, 1), w_ada, b_ada.reshape(n_layers, 1, 6 * d))

    p_gla = jnp.asarray(_gla_exponent_matrix(), BF16)
    tri = jnp.asarray(np.tril(np.ones((CHUNK, CHUNK), np.float32)), BF16)
    head_of_lane = np.arange(LANES) // RWKV_HEAD
    ones_bd = jnp.asarray((head_of_lane[:, None] == head_of_lane[None, :]).astype(np.float32), BF16)

    x2 = x.reshape(t, d)
    z_first = None
    for i in range(n_layers):
        j = i - 1
        w_pack, mu_pack = _pack_inproj(
            w_in[i], rwkv_mu[i],
            vres_w_down[j] if i > 0 else None, vres_mu[j] if i > 0 else None)
        z = _inproj_call(x2, g_pre_mix[i][None, :], mod, i, w_pack, mu_pack)
        if i == 0:
            z_first = z

        o_gla = _gla_call(
            z, _pad_rows(gla_w_a_up[i], LANES).astype(BF16), gla_b_a[i][None, :], gla_norm_w[i][None, :], p_gla)

        row = lambda a: a.reshape(1, RWKV_WIDTH)
        vec = dict(w0=row(rwkv_w0[i]), a0=row(rwkv_a0[i]), k_k=row(rwkv_k_k[i]), k_a=row(rwkv_k_a[i]),
                   r_k=row(rwkv_r_k[i]), gn_w=row(rwkv_gn_w[i]), gn_b=row(rwkv_gn_b[i]))
        mats = dict(w_up=_pad_rows(rwkv_w_up[i], LANES).astype(BF16),
                    a_up=_pad_rows(rwkv_a_up[i], LANES).astype(BF16),
                    g_up=rwkv_g_up[i].astype(BF16))
        if i > 0:
            vec["v0"] = row(vres_v0[j])
            mats["v_up"] = _pad_rows(vres_up[j], LANES).astype(BF16)
        o_rwkv = _rwkv_call(z, z_first, vec, mats, tri, ones_bd, has_vres=i > 0)

        w_o = w_out[i].astype(BF16)
        x2 = _outproj_call(x2, o_gla, o_rwkv, w_o[:GLA_V], w_o[GLA_V:], g_post_mix[i][None, :], mod, i)
        x2 = _ffn_call(x2, g_pre_ffn[i][None, :], g_post_ffn[i][None, :], mod, i,
                       w_ff1[i].astype(BF16), w_ff2[i].astype(BF16))
    return x2.reshape(bsz, t, d)
```

```python
import functools

import numpy as np
import jax
import jax.numpy as jnp
from jax import lax
from jax.experimental import pallas as pl
from jax.experimental.pallas import tpu as pltpu

F32 = jnp.float32
BF16 = jnp.bfloat16

D_MODEL = 2048
DEPTH = 2
GLA_V = 1024
GLA_DV = 128
GLA_HEADS = 8
GLA_DK = 64
GLA_QK = 512
GLA_GATE_RANK = 16
GLA_TAU = 16.0
RWKV_WIDTH = 1024
RWKV_HEAD = 64
RWKV_HEADS = 16
RWKV_W_RANK = 96
RWKV_A_RANK = 96
RWKV_G_RANK = 256
RWKV_V_RANK = 64
RWKV_GN_EPS = 64e-5
D_FF = 4 * D_MODEL
RMS_EPS = 1e-6

LANES = 128
CHUNK = 64
STACK = 2 * CHUNK
VMEM_LIMIT_BYTES = 56 * 1024 * 1024

C_GQ = 0
C_GK = 512
C_GV = 1024
C_GG = 2048
C_GA = 3072
C_VRES = 3200
C_RR = 3328
C_RK = 4352
C_RV = 5376
C_RG = 6400
C_RW = 6656
C_RA = 6784
N_PACK = 6912


def _dot(a, b):
    return jnp.dot(a, b, preferred_element_type=F32)


def _dot_nt(a, b):
    return lax.dot_general(a, b, (((1,), (1,)), ((), ())), preferred_element_type=F32)


def _dot_exact_lhs(p, x):
    hi = x.astype(BF16)
    r1 = x - hi.astype(F32)
    mid = r1.astype(BF16)
    lo = (r1 - mid.astype(F32)).astype(BF16)
    return _dot(p, hi) + _dot(p, mid) + _dot(p, lo)


def _dot_exact_rhs(x, p):
    hi = x.astype(BF16)
    r1 = x - hi.astype(F32)
    mid = r1.astype(BF16)
    lo = (r1 - mid.astype(F32)).astype(BF16)
    return _dot(hi, p) + _dot(mid, p) + _dot(lo, p)


def _log_sigmoid(x):
    return jnp.minimum(x, 0.0) - jnp.log1p(jnp.exp(-jnp.abs(x)))


def _sigmoid(x):
    return 1.0 / (1.0 + jnp.exp(-x))


def _stack_heads(x):
    lane = lax.broadcasted_iota(jnp.int32, x.shape, 1)
    first = lane < (LANES // 2)
    return jnp.concatenate([jnp.where(first, x, 0.0), jnp.where(first, 0.0, x)], axis=0)


ADA_TN = 1024


def _ada_kernel(c_ref, w_ref, b_ref, o_ref):
    c = c_ref[...]
    cond = c * _sigmoid(c)
    o_ref[0] = jnp.sum(w_ref[0] * cond, axis=0, keepdims=True) + b_ref[0]


def _ada_call(c_col, w_ada, b_ada3):
    n_layers, d, n = w_ada.shape
    return pl.pallas_call(
        _ada_kernel,
        out_shape=jax.ShapeDtypeStruct((n_layers, 1, n), F32),
        grid=(n_layers, n // ADA_TN),
        in_specs=[
            pl.BlockSpec((d, 1), lambda l, j: (0, 0)),
            pl.BlockSpec((1, d, ADA_TN), lambda l, j: (l, 0, j)),
            pl.BlockSpec((1, 1, ADA_TN), lambda l, j: (l, 0, j)),
        ],
        out_specs=pl.BlockSpec((1, 1, ADA_TN), lambda l, j: (l, 0, j)),
        compiler_params=pltpu.CompilerParams(
            dimension_semantics=("arbitrary", "arbitrary"), vmem_limit_bytes=VMEM_LIMIT_BYTES),
        name="ada_mod",
    )(c_col, w_ada, b_ada3)


INP_TM = 512
INP_TN = 768


def _inproj_kernel(x_ref, g_ref, sh_ref, sc_ref, w_ref, mu_ref, z_ref, h_scr, carry_scr):
    i = pl.program_id(0)
    j = pl.program_id(1)

    @pl.when(j == 0)
    def _():
        x = x_ref[...]
        ms = jnp.mean(x * x, axis=-1, keepdims=True)
        y = x * lax.rsqrt(ms + RMS_EPS) * g_ref[...]
        h_scr[...] = (y * (1.0 + sc_ref[0]) + sh_ref[0]).astype(BF16)

    @pl.when(i == 0)
    def _():
        carry_scr[j] = jnp.zeros(carry_scr.shape[1:], F32)

    z = _dot(h_scr[...], w_ref[...])
    prev = carry_scr[j]
    row = lax.broadcasted_iota(jnp.int32, z.shape, 0)
    shifted = jnp.where(row == 0, prev, pltpu.roll(z, 1, 0))
    carry_scr[j] = z[z.shape[0] - 1:, :]
    z_ref[...] = z + mu_ref[...] * (shifted - z)


def _inproj_call(x2, g_row, mod, layer, w_pack, mu_pack):
    t, d = x2.shape
    n = w_pack.shape[1]
    nj = n // INP_TN
    return pl.pallas_call(
        _inproj_kernel,
        out_shape=jax.ShapeDtypeStruct((t, n), F32),
        grid=(t // INP_TM, nj),
        in_specs=[
            pl.BlockSpec((INP_TM, d), lambda i, j: (i, 0)),
            pl.BlockSpec((1, d), lambda i, j: (0, 0)),
            pl.BlockSpec((1, 1, d), lambda i, j: (layer, 0, 0)),
            pl.BlockSpec((1, 1, d), lambda i, j: (layer, 0, 1)),
            pl.BlockSpec((d, INP_TN), lambda i, j: (0, j)),
            pl.BlockSpec((1, INP_TN), lambda i, j: (0, j)),
        ],
        out_specs=pl.BlockSpec((INP_TM, INP_TN), lambda i, j: (i, j)),
        scratch_shapes=[pltpu.VMEM((INP_TM, d), BF16), pltpu.VMEM((nj, 1, INP_TN), F32)],
        compiler_params=pltpu.CompilerParams(
            dimension_semantics=("arbitrary", "arbitrary"), vmem_limit_bytes=VMEM_LIMIT_BYTES),
        name="inproj",
    )(x2, g_row, mod, mod, w_pack, mu_pack)


GLA_NCH = 2
GLA_LEVELS = (32, 16, 8, 4, 2, 1)


def _gla_exponent_matrix():
    c = CHUNK
    p = np.zeros((2 * c + len(GLA_LEVELS) * c, c), np.float32)
    for i in range(c):
        p[i, : i + 1] = 1.0
        p[c + i, i + 1:] = 1.0
    for li, s in enumerate(GLA_LEVELS):
        base = 2 * c + li * c
        for i in range(c):
            m = (i // (2 * s)) * (2 * s) + s
            if i & s:
                p[base + i, m + 1: i + 1] = 1.0
            else:
                p[base + i, i + 1: m + 1] = 1.0
    return p


def _gla_kernel(q_ref, k_ref, v_ref, g_ref, al_ref, wup_ref, ba_ref, nw_ref, p_ref, o_ref, st_scr):
    @pl.when(pl.program_id(1) == 0)
    def _():
        st_scr[...] = jnp.zeros(st_scr.shape, F32)

    c = CHUNK
    row64 = lax.broadcasted_iota(jnp.int32, (c, LANES), 0)
    rr = lax.broadcasted_iota(jnp.int32, (STACK, STACK), 0)
    cc = lax.broadcasted_iota(jnp.int32, (STACK, STACK), 1)
    nw = nw_ref[...]

    for ci in range(GLA_NCH):
        sl = pl.ds(ci * c, c)
        q = q_ref[sl, :] * (GLA_DK ** -0.5)
        k = k_ref[sl, :]
        v = v_ref[sl, :]
        g = g_ref[sl, :]
        x = _dot(al_ref[sl, :].astype(BF16), wup_ref[...]) + ba_ref[...]
        la = _log_sigmoid(x) * (1.0 / GLA_TAU)
        e_all = _dot_exact_lhs(p_ref[...], la)
        b = e_all[0:c]
        b_rest = e_all[c:2 * c]
        b_last = b[c - 1:c, :]

        v_cat = jnp.concatenate([v[:, :LANES], v[:, LANES:]], axis=0)
        v_cat_b = v_cat.astype(BF16)

        q_st = _stack_heads(q)
        k_st = _stack_heads(k)
        scores = jnp.where(rr == cc, _dot_nt(q_st.astype(BF16), k_st.astype(BF16)), 0.0)
        for li, s in enumerate(GLA_LEVELS):
            e = jnp.exp(e_all[(2 + li) * c:(3 + li) * c])
            second = (row64 & s) != 0
            qd = _stack_heads(jnp.where(second, q * e, 0.0)).astype(BF16)
            kd = _stack_heads(jnp.where(second, 0.0, k * e)).astype(BF16)
            shift = int(np.log2(2 * s))
            same = (rr >> shift) == (cc >> shift)
            scores = scores + jnp.where(same, _dot_nt(qd, kd), 0.0)
        o_intra = _dot(scores.astype(BF16), v_cat_b)

        st = st_scr[...]
        qe = _stack_heads(q * jnp.exp(b)).astype(BF16)
        o_inter = _dot_nt(qe, st.astype(BF16))
        ke = _stack_heads(k * jnp.exp(b_rest)).astype(BF16)
        st_scr[...] = st * jnp.exp(b_last) + _dot(v_cat.T.astype(BF16), ke)

        o = o_inter + o_intra
        o = o * lax.rsqrt(jnp.mean(o * o, axis=-1, keepdims=True) + RMS_EPS)
        gs = g * _sigmoid(g)
        o0 = o[:c] * nw[:, :LANES] * gs[:, :LANES]
        o1 = o[c:] * nw[:, LANES:] * gs[:, LANES:]
        o_ref[sl, :] = jnp.concatenate([o0, o1], axis=1).astype(o_ref.dtype)


def _gla_call(z, wup_pad, ba_row, nw_row, p_mat):
    t = z.shape[0]
    tb = CHUNK * GLA_NCH
    pairs = GLA_HEADS // 2
    blk = lambda w, off: pl.BlockSpec((tb, w), lambda p, c: (c, off // w + p))
    return pl.pallas_call(
        _gla_kernel,
        out_shape=jax.ShapeDtypeStruct((t, GLA_V), BF16),
        grid=(pairs, t // tb),
        in_specs=[
            blk(LANES, C_GQ), blk(LANES, C_GK), blk(2 * LANES, C_GV), blk(2 * LANES, C_GG),
            pl.BlockSpec((tb, LANES), lambda p, c: (c, C_GA // LANES)),
            pl.BlockSpec((LANES, LANES), lambda p, c: (0, p)),
            pl.BlockSpec((1, LANES), lambda p, c: (0, p)),
            pl.BlockSpec((1, 2 * LANES), lambda p, c: (0, p)),
            pl.BlockSpec(p_mat.shape, lambda p, c: (0, 0)),
        ],
        out_specs=pl.BlockSpec((tb, 2 * LANES), lambda p, c: (c, p)),
        scratch_shapes=[pltpu.VMEM((GLA_DV, LANES), F32)],
        compiler_params=pltpu.CompilerParams(
            dimension_semantics=("arbitrary", "arbitrary"), vmem_limit_bytes=VMEM_LIMIT_BYTES),
        name="gla_mixer",
    )(z, z, z, z, z, wup_pad, ba_row, nw_row, p_mat)


RWKV_NCH = 4
RWKV_GROUP = 4
RWKV_GW = RWKV_GROUP * RWKV_HEAD
RWKV_ST = RWKV_GROUP * CHUNK
RWKV_INV_LEVELS = (2, 4, 8, 16, 32)


def _stack_group(x):
    head = lax.broadcasted_iota(jnp.int32, x.shape, 1) >> int(np.log2(RWKV_HEAD))
    return jnp.concatenate([jnp.where(head == h, x, 0.0) for h in range(RWKV_GROUP)], axis=0)


def _rwkv_kernel(*refs, has_vres):
    if has_vres:
        (r_ref, k_ref, v_ref, wl_ref, al_ref, gl_ref, vl_ref, vf_ref, vup_ref, v0_ref,
         w0_ref, wup_ref, a0_ref, aup_ref, gup_ref, kk_ref, ka_ref, rk_ref, gnw_ref, gnb_ref,
         tri_ref, ones_ref, o_ref, st_scr) = refs
    else:
        (r_ref, k_ref, v_ref, wl_ref, al_ref, gl_ref,
         w0_ref, wup_ref, a0_ref, aup_ref, gup_ref, kk_ref, ka_ref, rk_ref, gnw_ref, gnb_ref,
         tri_ref, ones_ref, o_ref, st_scr) = refs

    @pl.when(pl.program_id(1) == 0)
    def _():
        st_scr[...] = jnp.zeros(st_scr.shape, F32)

    c = CHUNK
    n = RWKV_ST
    rr = lax.broadcasted_iota(jnp.int32, (n, n), 0)
    cc = lax.broadcasted_iota(jnp.int32, (n, n), 1)
    strict = rr > cc
    incl = rr >= cc
    ones_bd = ones_ref[...]
    inv_n = 1.0 / RWKV_HEAD

    def seg_sum(x):
        return _dot_exact_rhs(x, ones_bd)

    chunks = range(RWKV_NCH)
    pre = []
    for ci in chunks:
        sl = pl.ds(ci * c, c)
        r = r_ref[sl, :]
        k = k_ref[sl, :]
        v = v_ref[sl, :]
        w_pre = w0_ref[...] + _dot(jnp.tanh(wl_ref[sl, :]).astype(BF16), wup_ref[...])
        lw = -jnp.exp(_log_sigmoid(w_pre) - 0.5)
        alr = _sigmoid(a0_ref[...] + _dot(al_ref[sl, :].astype(BF16), aup_ref[...]))
        gate = _dot(_sigmoid(gl_ref[sl, :]).astype(BF16), gup_ref[...])
        if has_vres:
            mix = _sigmoid(v0_ref[...] + _dot(vl_ref[sl, :].astype(BF16), vup_ref[...]))
            v = v + (vf_ref[sl, :] - v) * mix
        kk = k * kk_ref[...]
        kk = kk * lax.rsqrt(jnp.maximum(seg_sum(kk * kk), 1e-24))
        k2 = k * (1.0 + (alr - 1.0) * ka_ref[...])
        a = -kk
        b = kk * alr

        cl = _dot_exact_lhs(tri_ref[...], lw)
        cl_last = cl[c - 1:c, :]
        e_neg = jnp.exp(-cl)
        e_end = jnp.exp(cl_last - cl)
        at_s = _stack_group(a * jnp.exp(cl - lw)).astype(BF16)
        rt_s = _stack_group(r * jnp.exp(cl))
        bt_s = _stack_group(b * e_neg).astype(BF16)
        kt_s = _stack_group(k2 * e_neg).astype(BF16)
        bh_s = _stack_group(b * e_end).astype(BF16)
        kh_s = _stack_group(k2 * e_end).astype(BF16)
        v_m = _stack_group(v)
        v_mb = v_m.astype(BF16)

        lhs = jnp.concatenate([at_s, rt_s.astype(BF16)], axis=0)
        rhs = jnp.concatenate([bt_s, kt_s], axis=0)
        aa = _dot_nt(lhs, rhs)
        pre.append(dict(
            sl=sl, at_s=at_s, rt_s=rt_s, bh_s=bh_s, kh_s=kh_s, v_m=v_m, v_mb=v_mb, g_c=jnp.exp(cl_last),
            a_ab=jnp.where(strict, aa[:n, :n], 0.0),
            a_ak=jnp.where(strict, aa[:n, n:], 0.0).astype(BF16),
            p_rb=jnp.where(incl, aa[n:, :n], 0.0).astype(BF16),
            p_rk=jnp.where(incl, aa[n:, n:], 0.0).astype(BF16),
            post=(gate, seg_sum(r * k2 * rk_ref[...]) * v)))

    tinv = [jnp.where(rr == cc, 1.0, jnp.where((rr >> 1) == (cc >> 1), p["a_ab"], 0.0)) for p in pre]
    for s in RWKV_INV_LEVELS:
        shift = int(np.log2(2 * s))
        off = ((rr >> shift) == (cc >> shift)) & ((rr & s) != 0) & ((cc & s) == 0)
        t_b = [t.astype(BF16) for t in tinv]
        half = [_dot(t_b[ci], jnp.where(off, pre[ci]["a_ab"], 0.0).astype(BF16)).astype(BF16) for ci in chunks]
        tinv = [tinv[ci] + _dot(half[ci], t_b[ci]) for ci in chunks]

    mid = []
    for ci in chunks:
        p = pre[ci]
        t_b = tinv[ci].astype(BF16)
        wt = _dot(t_b, p["at_s"])
        u0 = _dot(t_b, _dot(p["a_ak"], p["v_mb"]).astype(BF16))
        qh = p["rt_s"] + _dot(p["p_rb"], wt.astype(BF16))
        y0 = _dot(p["p_rb"], u0.astype(BF16)) + _dot(p["p_rk"], p["v_mb"])
        gmat = _dot(wt.T.astype(BF16), p["bh_s"])
        n0t = _dot(u0.T.astype(BF16), p["bh_s"]) + _dot(p["v_m"].T.astype(BF16), p["kh_s"])
        mid.append((qh.astype(BF16), y0, gmat.astype(BF16), n0t))

    st = st_scr[...]
    for ci in chunks:
        qh_b, y0, gmat_b, n0t = mid[ci]
        st_b = st.astype(BF16)
        ym = _dot_nt(qh_b, st_b) + y0
        st = st * pre[ci]["g_c"] + _dot(st_b, gmat_b) + n0t

        y = ym[0:c] + ym[c:2 * c] + ym[2 * c:3 * c] + ym[3 * c:4 * c]
        mean = seg_sum(y) * inv_n
        yc = y - mean
        var = seg_sum(yc * yc) * inv_n
        yn = yc * lax.rsqrt(var + RWKV_GN_EPS) * gnw_ref[...] + gnb_ref[...]
        gate, bonus = pre[ci]["post"]
        o_ref[pre[ci]["sl"], :] = ((yn + bonus) * gate).astype(o_ref.dtype)
    st_scr[...] = st


def _rwkv_call(z, z_first, vec, mats, tri, ones_bd, has_vres):
    t = z.shape[0]
    tb = CHUNK * RWKV_NCH
    gw = RWKV_GW
    groups = RWKV_HEADS // RWKV_GROUP
    col = lambda off: pl.BlockSpec((tb, gw), lambda p, c: (c, off // gw + p))
    fixed = lambda off, w: pl.BlockSpec((tb, w), lambda p, c: (c, off // w))
    rowv = pl.BlockSpec((1, gw), lambda p, c: (0, p))
    upm = lambda rows: pl.BlockSpec((rows, gw), lambda p, c: (0, p))
    const = lambda a: pl.BlockSpec(a.shape, lambda p, c: (0, 0))

    in_specs = [col(C_RR), col(C_RK), col(C_RV), fixed(C_RW, LANES), fixed(C_RA, LANES), fixed(C_RG, 2 * LANES)]
    args = [z, z, z, z, z, z]
    if has_vres:
        in_specs += [fixed(C_VRES, LANES), col(C_RV), upm(LANES), rowv]
        args += [z, z_first, mats["v_up"], vec["v0"]]
    in_specs += [rowv, upm(LANES), rowv, upm(LANES), upm(2 * LANES), rowv, rowv, rowv, rowv, rowv,
                 const(tri), const(ones_bd)]
    args += [vec["w0"], mats["w_up"], vec["a0"], mats["a_up"], mats["g_up"], vec["k_k"], vec["k_a"],
             vec["r_k"], vec["gn_w"], vec["gn_b"], tri, ones_bd]
    return pl.pallas_call(
        functools.partial(_rwkv_kernel, has_vres=has_vres),
        out_shape=jax.ShapeDtypeStruct((t, RWKV_WIDTH), BF16),
        grid=(groups, t // tb),
        in_specs=in_specs,
        out_specs=pl.BlockSpec((tb, gw), lambda p, c: (c, p)),
        scratch_shapes=[pltpu.VMEM((RWKV_ST, RWKV_ST), F32)],
        compiler_params=pltpu.CompilerParams(
            dimension_semantics=("arbitrary", "arbitrary"), vmem_limit_bytes=VMEM_LIMIT_BYTES),
        name="rwkv7_mixer",
    )(*args)


OUT_TM = 256


def _outproj_kernel(x_ref, oa_ref, ob_ref, wa_ref, wb_ref, g_ref, gt_ref, o_ref):
    y = _dot(oa_ref[...], wa_ref[...]) + _dot(ob_ref[...], wb_ref[...])
    ms = jnp.mean(y * y, axis=-1, keepdims=True)
    o_ref[...] = x_ref[...] + gt_ref[0] * (y * lax.rsqrt(ms + RMS_EPS) * g_ref[...])


def _outproj_call(x2, o_gla, o_rwkv, w_a, w_b, g_row, mod, layer):
    t, d = x2.shape
    return pl.pallas_call(
        _outproj_kernel,
        out_shape=jax.ShapeDtypeStruct((t, d), F32),
        grid=(t // OUT_TM,),
        in_specs=[
            pl.BlockSpec((OUT_TM, d), lambda i: (i, 0)),
            pl.BlockSpec((OUT_TM, GLA_V), lambda i: (i, 0)),
            pl.BlockSpec((OUT_TM, RWKV_WIDTH), lambda i: (i, 0)),
            pl.BlockSpec((GLA_V, d), lambda i: (0, 0)),
            pl.BlockSpec((RWKV_WIDTH, d), lambda i: (0, 0)),
            pl.BlockSpec((1, d), lambda i: (0, 0)),
            pl.BlockSpec((1, 1, d), lambda i: (layer, 0, 2)),
        ],
        out_specs=pl.BlockSpec((OUT_TM, d), lambda i: (i, 0)),
        compiler_params=pltpu.CompilerParams(
            dimension_semantics=("arbitrary",), vmem_limit_bytes=VMEM_LIMIT_BYTES),
        name="outproj",
    )(x2, o_gla, o_rwkv, w_a, w_b, g_row, mod)


FFN_TM = 512
FFN_TF = 1024


def _ffn_kernel(x_ref, gpre_ref, sh_ref, sc_ref, w1_ref, w2_ref, gpost_ref, gt_ref, o_ref, h_scr, acc_scr):
    f = pl.program_id(1)

    @pl.when(f == 0)
    def _():
        x = x_ref[...]
        ms = jnp.mean(x * x, axis=-1, keepdims=True)
        y = x * lax.rsqrt(ms + RMS_EPS) * gpre_ref[...]
        h_scr[...] = (y * (1.0 + sc_ref[0]) + sh_ref[0]).astype(BF16)
        acc_scr[...] = jnp.zeros(acc_scr.shape, F32)

    u = jnp.maximum(_dot(h_scr[...], w1_ref[...]), 0.0)
    acc_scr[...] += _dot((u * u).astype(BF16), w2_ref[...])

    @pl.when(f == pl.num_programs(1) - 1)
    def _():
        y = acc_scr[...]
        ms = jnp.mean(y * y, axis=-1, keepdims=True)
        o_ref[...] = x_ref[...] + gt_ref[0] * (y * lax.rsqrt(ms + RMS_EPS) * gpost_ref[...])


def _ffn_call(x2, gpre_row, gpost_row, mod, layer, w1, w2):
    t, d = x2.shape
    dff = w1.shape[1]
    return pl.pallas_call(
        _ffn_kernel,
        out_shape=jax.ShapeDtypeStruct((t, d), F32),
        grid=(t // FFN_TM, dff // FFN_TF),
        in_specs=[
            pl.BlockSpec((FFN_TM, d), lambda i, f: (i, 0)),
            pl.BlockSpec((1, d), lambda i, f: (0, 0)),
            pl.BlockSpec((1, 1, d), lambda i, f: (layer, 0, 3)),
            pl.BlockSpec((1, 1, d), lambda i, f: (layer, 0, 4)),
            pl.BlockSpec((d, FFN_TF), lambda i, f: (0, f)),
            pl.BlockSpec((FFN_TF, d), lambda i, f: (f, 0)),
            pl.BlockSpec((1, d), lambda i, f: (0, 0)),
            pl.BlockSpec((1, 1, d), lambda i, f: (layer, 0, 5)),
        ],
        out_specs=pl.BlockSpec((FFN_TM, d), lambda i, f: (i, 0)),
        scratch_shapes=[pltpu.VMEM((FFN_TM, d), BF16), pltpu.VMEM((FFN_TM, d), F32)],
        compiler_params=pltpu.CompilerParams(
            dimension_semantics=("arbitrary", "arbitrary"), vmem_limit_bytes=VMEM_LIMIT_BYTES),
        name="ffn",
    )(x2, gpre_row, mod, mod, w1, w2, gpost_row, mod)


def _pad_cols(w, width):
    return jnp.pad(w, ((0, 0), (0, width - w.shape[1])))


def _pad_rows(w, rows):
    return jnp.pad(w, ((0, rows - w.shape[0]), (0, 0)))


def _pack_inproj(w_in, mu_rwkv, vres_w_down, vres_mu):
    d = w_in.shape[0]
    gq, gk, gv, gg, ga, rr, rk, rv, rw, ra, rg = jnp.split(
        w_in, np.cumsum([512, 512, 1024, 1024, 16, 1024, 1024, 1024, 96, 96]).tolist(), axis=1)
    mr, mk, mv, mw, ma, mg = jnp.split(mu_rwkv[None, :], np.cumsum([1024, 1024, 1024, 96, 96]).tolist(), axis=1)
    if vres_w_down is None:
        vres_w_down = jnp.zeros((d, RWKV_V_RANK), w_in.dtype)
        vres_mu = jnp.zeros((RWKV_V_RANK,), w_in.dtype)
    w_pack = jnp.concatenate(
        [gq, gk, gv, gg, _pad_cols(ga, LANES), _pad_cols(vres_w_down, LANES),
         rr, rk, rv, rg, _pad_cols(rw, LANES), _pad_cols(ra, LANES)], axis=1).astype(BF16)
    mu_pack = jnp.concatenate(
        [jnp.zeros((1, C_VRES), F32), _pad_cols(vres_mu[None, :], LANES),
         mr, mk, mv, mg, _pad_cols(mw, LANES), _pad_cols(ma, LANES)], axis=1)
    return w_pack, mu_pack


def kernel(x, c, w_ada, b_ada, g_pre_mix, g_post_mix, g_pre_ffn, g_post_ffn, w_in, gla_w_a_up, gla_b_a, gla_norm_w, rwkv_mu, rwkv_w0, rwkv_w_up, rwkv_a0, rwkv_a_up, rwkv_g_up, rwkv_k_k, rwkv_k_a, rwkv_r_k, rwkv_gn_w, rwkv_gn_b, vres_w_down, vres_mu, vres_up, vres_v0, w_out, w_ff1, w_ff2):
    bsz, t, d = x.shape
    assert bsz == 1 and d == D_MODEL and t % (CHUNK * max(GLA_NCH, RWKV_NCH)) == 0 and t % FFN_TM == 0
    n_layers = w_ada.shape[0]

    mod = _ada_call(c.reshape(d, 1), w_ada, b_ada.reshape(n_layers, 1, 6 * d))

    p_gla = jnp.asarray(_gla_exponent_matrix(), BF16)
    tri = jnp.asarray(np.tril(np.ones((CHUNK, CHUNK), np.float32)), BF16)
    head_of_lane = np.arange(RWKV_GW) // RWKV_HEAD
    ones_bd = jnp.asarray((head_of_lane[:, None] == head_of_lane[None, :]).astype(np.float32), BF16)

    x2 = x.reshape(t, d)
    z_first = None
    for i in range(n_layers):
        j = i - 1
        w_pack, mu_pack = _pack_inproj(
            w_in[i], rwkv_mu[i],
            vres_w_down[j] if i > 0 else None, vres_mu[j] if i > 0 else None)
        z = _inproj_call(x2, g_pre_mix[i][None, :], mod, i, w_pack, mu_pack)
        if i == 0:
            z_first = z

        o_gla = _gla_call(
            z, _pad_rows(gla_w_a_up[i], LANES).astype(BF16), gla_b_a[i][None, :], gla_norm_w[i][None, :], p_gla)

        row = lambda a: a.reshape(1, RWKV_WIDTH)
        vec = dict(w0=row(rwkv_w0[i]), a0=row(rwkv_a0[i]), k_k=row(rwkv_k_k[i]), k_a=row(rwkv_k_a[i]),
                   r_k=row(rwkv_r_k[i]), gn_w=row(rwkv_gn_w[i]), gn_b=row(rwkv_gn_b[i]))
        mats = dict(w_up=_pad_rows(rwkv_w_up[i], LANES).astype(BF16),
                    a_up=_pad_rows(rwkv_a_up[i], LANES).astype(BF16),
                    g_up=rwkv_g_up[i].astype(BF16))
        if i > 0:
            vec["v0"] = row(vres_v0[j])
            mats["v_up"] = _pad_rows(vres_up[j], LANES).astype(BF16)
        o_rwkv = _rwkv_call(z, z_first, vec, mats, tri, ones_bd, has_vres=i > 0)

        w_o = w_out[i].astype(BF16)
        x2 = _outproj_call(x2, o_gla, o_rwkv, w_o[:GLA_V], w_o[GLA_V:], g_post_mix[i][None, :], mod, i)
        x2 = _ffn_call(x2, g_pre_ffn[i][None, :], g_post_ffn[i][None, :], mod, i,
                       w_ff1[i].astype(BF16), w_ff2[i].astype(BF16))
    return x2.reshape(bsz, t, d)
```

```python
import functools

import numpy as np
import jax
import jax.numpy as jnp
from jax import lax
from jax.experimental import pallas as pl
from jax.experimental.pallas import tpu as pltpu

F32 = jnp.float32
BF16 = jnp.bfloat16

D_MODEL = 2048
DEPTH = 2
GLA_V = 1024
GLA_DV = 128
GLA_HEADS = 8
GLA_DK = 64
GLA_QK = 512
GLA_GATE_RANK = 16
GLA_TAU = 16.0
RWKV_WIDTH = 1024
RWKV_HEAD = 64
RWKV_HEADS = 16
RWKV_W_RANK = 96
RWKV_A_RANK = 96
RWKV_G_RANK = 256
RWKV_V_RANK = 64
RWKV_GN_EPS = 64e-5
D_FF = 4 * D_MODEL
RMS_EPS = 1e-6

LANES = 128
CHUNK = 64
STACK = 2 * CHUNK
VMEM_LIMIT_BYTES = 56 * 1024 * 1024

C_GQ = 0
C_GK = 512
C_GV = 1024
C_GG = 2048
C_GA = 3072
C_VRES = 3200
C_RR = 3328
C_RK = 4352
C_RV = 5376
C_RG = 6400
C_RW = 6656
C_RA = 6784
N_PACK = 6912


def _dot(a, b):
    return jnp.dot(a, b, preferred_element_type=F32)


def _dot_nt(a, b):
    return lax.dot_general(a, b, (((1,), (1,)), ((), ())), preferred_element_type=F32)


def _dot_exact_lhs(p, x):
    hi = x.astype(BF16)
    r1 = x - hi.astype(F32)
    mid = r1.astype(BF16)
    lo = (r1 - mid.astype(F32)).astype(BF16)
    return _dot(p, hi) + _dot(p, mid) + _dot(p, lo)


def _dot_exact_rhs(x, p):
    hi = x.astype(BF16)
    r1 = x - hi.astype(F32)
    mid = r1.astype(BF16)
    lo = (r1 - mid.astype(F32)).astype(BF16)
    return _dot(hi, p) + _dot(mid, p) + _dot(lo, p)


def _log_sigmoid(x):
    return jnp.minimum(x, 0.0) - jnp.log(1.0 + jnp.exp(-jnp.abs(x)))


def _sigmoid(x):
    return 1.0 / (1.0 + jnp.exp(-x))


def _stack_heads(x):
    lane = lax.broadcasted_iota(jnp.int32, x.shape, 1)
    first = lane < (LANES // 2)
    return jnp.concatenate([jnp.where(first, x, 0.0), jnp.where(first, 0.0, x)], axis=0)


ADA_TN = 1024


def _ada_kernel(c_ref, w_ref, b_ref, o_ref):
    c = c_ref[...]
    cond = c * _sigmoid(c)
    o_ref[0] = jnp.sum(w_ref[0] * cond, axis=0, keepdims=True) + b_ref[0]


def _ada_call(c_col, w_ada, b_ada3):
    n_layers, d, n = w_ada.shape
    return pl.pallas_call(
        _ada_kernel,
        out_shape=jax.ShapeDtypeStruct((n_layers, 1, n), F32),
        grid=(n_layers, n // ADA_TN),
        in_specs=[
            pl.BlockSpec((d, 1), lambda l, j: (0, 0)),
            pl.BlockSpec((1, d, ADA_TN), lambda l, j: (l, 0, j)),
            pl.BlockSpec((1, 1, ADA_TN), lambda l, j: (l, 0, j)),
        ],
        out_specs=pl.BlockSpec((1, 1, ADA_TN), lambda l, j: (l, 0, j)),
        compiler_params=pltpu.CompilerParams(
            dimension_semantics=("arbitrary", "arbitrary"), vmem_limit_bytes=VMEM_LIMIT_BYTES),
        name="ada_mod",
    )(c_col, w_ada, b_ada3)


INP_TM = 512
INP_TN = 768


def _inproj_kernel(x_ref, g_ref, sh_ref, sc_ref, w_ref, mu_ref, z_ref, h_scr, carry_scr):
    i = pl.program_id(0)
    j = pl.program_id(1)

    @pl.when(j == 0)
    def _():
        x = x_ref[...]
        ms = jnp.mean(x * x, axis=-1, keepdims=True)
        y = x * lax.rsqrt(ms + RMS_EPS) * g_ref[...]
        h_scr[...] = (y * (1.0 + sc_ref[0]) + sh_ref[0]).astype(BF16)

    @pl.when(i == 0)
    def _():
        carry_scr[j] = jnp.zeros(carry_scr.shape[1:], F32)

    z = _dot(h_scr[...], w_ref[...])
    prev = carry_scr[j]
    row = lax.broadcasted_iota(jnp.int32, z.shape, 0)
    shifted = jnp.where(row == 0, prev, pltpu.roll(z, 1, 0))
    carry_scr[j] = z[z.shape[0] - 1:, :]
    z_ref[...] = z + mu_ref[...] * (shifted - z)


def _inproj_call(x2, g_row, mod, layer, w_pack, mu_pack):
    t, d = x2.shape
    n = w_pack.shape[1]
    nj = n // INP_TN
    return pl.pallas_call(
        _inproj_kernel,
        out_shape=jax.ShapeDtypeStruct((t, n), F32),
        grid=(t // INP_TM, nj),
        in_specs=[
            pl.BlockSpec((INP_TM, d), lambda i, j: (i, 0)),
            pl.BlockSpec((1, d), lambda i, j: (0, 0)),
            pl.BlockSpec((1, 1, d), lambda i, j: (layer, 0, 0)),
            pl.BlockSpec((1, 1, d), lambda i, j: (layer, 0, 1)),
            pl.BlockSpec((d, INP_TN), lambda i, j: (0, j)),
            pl.BlockSpec((1, INP_TN), lambda i, j: (0, j)),
        ],
        out_specs=pl.BlockSpec((INP_TM, INP_TN), lambda i, j: (i, j)),
        scratch_shapes=[pltpu.VMEM((INP_TM, d), BF16), pltpu.VMEM((nj, 1, INP_TN), F32)],
        compiler_params=pltpu.CompilerParams(
            dimension_semantics=("arbitrary", "arbitrary"), vmem_limit_bytes=VMEM_LIMIT_BYTES),
        name="inproj",
    )(x2, g_row, mod, mod, w_pack, mu_pack)


GLA_NCH = 2
GLA_LEVELS = (32, 16, 8, 4, 2, 1)


def _gla_exponent_matrix():
    c = CHUNK
    p = np.zeros((2 * c + len(GLA_LEVELS) * c, c), np.float32)
    for i in range(c):
        p[i, : i + 1] = 1.0
        p[c + i, i + 1:] = 1.0
    for li, s in enumerate(GLA_LEVELS):
        base = 2 * c + li * c
        for i in range(c):
            m = (i // (2 * s)) * (2 * s) + s
            if i & s:
                p[base + i, m + 1: i + 1] = 1.0
            else:
                p[base + i, i + 1: m + 1] = 1.0
    return p


def _gla_kernel(q_ref, k_ref, v_ref, g_ref, al_ref, wup_ref, ba_ref, nw_ref, p_ref, o_ref, st_scr):
    @pl.when(pl.program_id(1) == 0)
    def _():
        st_scr[...] = jnp.zeros(st_scr.shape, F32)

    c = CHUNK
    row64 = lax.broadcasted_iota(jnp.int32, (c, LANES), 0)
    rr = lax.broadcasted_iota(jnp.int32, (STACK, STACK), 0)
    cc = lax.broadcasted_iota(jnp.int32, (STACK, STACK), 1)
    nw = nw_ref[...]

    for ci in range(GLA_NCH):
        sl = pl.ds(ci * c, c)
        q = q_ref[sl, :] * (GLA_DK ** -0.5)
        k = k_ref[sl, :]
        v = v_ref[sl, :]
        g = g_ref[sl, :]
        x = _dot(al_ref[sl, :].astype(BF16), wup_ref[...]) + ba_ref[...]
        la = _log_sigmoid(x) * (1.0 / GLA_TAU)
        e_all = _dot_exact_lhs(p_ref[...], la)
        b = e_all[0:c]
        b_rest = e_all[c:2 * c]
        b_last = b[c - 1:c, :]

        v_cat = jnp.concatenate([v[:, :LANES], v[:, LANES:]], axis=0)
        v_cat_b = v_cat.astype(BF16)

        q_st = _stack_heads(q)
        k_st = _stack_heads(k)
        scores = jnp.where(rr == cc, _dot_nt(q_st.astype(BF16), k_st.astype(BF16)), 0.0)
        for li, s in enumerate(GLA_LEVELS):
            e = jnp.exp(e_all[(2 + li) * c:(3 + li) * c])
            second = (row64 & s) != 0
            qd = _stack_heads(jnp.where(second, q * e, 0.0)).astype(BF16)
            kd = _stack_heads(jnp.where(second, 0.0, k * e)).astype(BF16)
            shift = int(np.log2(2 * s))
            same = (rr >> shift) == (cc >> shift)
            scores = scores + jnp.where(same, _dot_nt(qd, kd), 0.0)
        o_intra = _dot(scores.astype(BF16), v_cat_b)

        st = st_scr[...]
        qe = _stack_heads(q * jnp.exp(b)).astype(BF16)
        o_inter = _dot_nt(qe, st.astype(BF16))
        ke = _stack_heads(k * jnp.exp(b_rest)).astype(BF16)
        st_scr[...] = st * jnp.exp(b_last) + _dot(v_cat.T.astype(BF16), ke)

        o = o_inter + o_intra
        o = o * lax.rsqrt(jnp.mean(o * o, axis=-1, keepdims=True) + RMS_EPS)
        gs = g * _sigmoid(g)
        o0 = o[:c] * nw[:, :LANES] * gs[:, :LANES]
        o1 = o[c:] * nw[:, LANES:] * gs[:, LANES:]
        o_ref[sl, :] = jnp.concatenate([o0, o1], axis=1).astype(o_ref.dtype)


def _gla_call(z, wup_pad, ba_row, nw_row, p_mat):
    t = z.shape[0]
    tb = CHUNK * GLA_NCH
    pairs = GLA_HEADS // 2
    blk = lambda w, off: pl.BlockSpec((tb, w), lambda p, c: (c, off // w + p))
    return pl.pallas_call(
        _gla_kernel,
        out_shape=jax.ShapeDtypeStruct((t, GLA_V), BF16),
        grid=(pairs, t // tb),
        in_specs=[
            blk(LANES, C_GQ), blk(LANES, C_GK), blk(2 * LANES, C_GV), blk(2 * LANES, C_GG),
            pl.BlockSpec((tb, LANES), lambda p, c: (c, C_GA // LANES)),
            pl.BlockSpec((LANES, LANES), lambda p, c: (0, p)),
            pl.BlockSpec((1, LANES), lambda p, c: (0, p)),
            pl.BlockSpec((1, 2 * LANES), lambda p, c: (0, p)),
            pl.BlockSpec(p_mat.shape, lambda p, c: (0, 0)),
        ],
        out_specs=pl.BlockSpec((tb, 2 * LANES), lambda p, c: (c, p)),
        scratch_shapes=[pltpu.VMEM((GLA_DV, LANES), F32)],
        compiler_params=pltpu.CompilerParams(
            dimension_semantics=("arbitrary", "arbitrary"), vmem_limit_bytes=VMEM_LIMIT_BYTES),
        name="gla_mixer",
    )(z, z, z, z, z, wup_pad, ba_row, nw_row, p_mat)


RWKV_NCH = 8
RWKV_GROUP = 4
RWKV_GW = RWKV_GROUP * RWKV_HEAD
RWKV_INV_LEVELS = (2, 4, 8, 16, 32)


def _dot_split2_lhs(p, x):
    hi = x.astype(BF16)
    lo = (x - hi.astype(F32)).astype(BF16)
    y = _dot(p, jnp.concatenate([hi, lo], axis=1))
    return y[:, :x.shape[1]] + y[:, x.shape[1]:]


def _dot_split2_rhs(x, p):
    hi = x.astype(BF16)
    lo = (x - hi.astype(F32)).astype(BF16)
    y = _dot(jnp.concatenate([hi, lo], axis=0), p)
    return y[:x.shape[0]] + y[x.shape[0]:]


def _rwkv_kernel(*refs, has_vres):
    if has_vres:
        (r_ref, k_ref, v_ref, wl_ref, al_ref, gl_ref, vl_ref, vf_ref, vup_ref, v0_ref,
         w0_ref, wup_ref, a0_ref, aup_ref, gup_ref, kk_ref, ka_ref, rk_ref, gnw_ref, gnb_ref,
         tri_ref, ones_ref, o_ref, st_scr) = refs
    else:
        (r_ref, k_ref, v_ref, wl_ref, al_ref, gl_ref,
         w0_ref, wup_ref, a0_ref, aup_ref, gup_ref, kk_ref, ka_ref, rk_ref, gnw_ref, gnb_ref,
         tri_ref, ones_ref, o_ref, st_scr) = refs

    @pl.when(pl.program_id(1) == 0)
    def _():
        st_scr[...] = jnp.zeros(st_scr.shape, F32)

    c = CHUNK
    gw = RWKV_GW
    tt = lax.broadcasted_iota(jnp.int32, (c, gw), 0)
    jj = lax.broadcasted_iota(jnp.int32, (c, gw), 1) & (RWKV_HEAD - 1)
    strict = tt > jj
    incl = tt >= jj
    ones_bd = ones_ref[...]
    same_head_b = ones_bd != 0
    hshift = int(np.log2(RWKV_HEAD))
    same_head = ((lax.broadcasted_iota(jnp.int32, (gw, gw), 0) >> hshift)
                 == (lax.broadcasted_iota(jnp.int32, (gw, gw), 1) >> hshift))
    inv_n = 1.0 / RWKV_HEAD

    def seg_sum(x):
        return _dot_split2_rhs(x, ones_bd)

    def block_diag(x):
        return jnp.concatenate([x.astype(BF16)] * RWKV_GROUP, axis=0) * ones_bd

    tb = c * RWKV_NCH
    r_all = r_ref[...]
    k_all = k_ref[...]
    v_all = v_ref[...]
    w_pre = w0_ref[...] + _dot(jnp.tanh(wl_ref[...]).astype(BF16), wup_ref[...])
    lw_all = -jnp.exp(_log_sigmoid(w_pre) - 0.5)
    alr = _sigmoid(a0_ref[...] + _dot(al_ref[...].astype(BF16), aup_ref[...]))
    gate_all = _dot(_sigmoid(gl_ref[...]).astype(BF16), gup_ref[...])
    if has_vres:
        mix = _sigmoid(v0_ref[...] + _dot(vl_ref[...].astype(BF16), vup_ref[...]))
        v_all = v_all + (vf_ref[...] - v_all) * mix
    kk = k_all * kk_ref[...]
    k2_all = k_all * (1.0 + (alr - 1.0) * ka_ref[...])
    sums = seg_sum(jnp.concatenate([kk * kk, r_all * k2_all * rk_ref[...]], axis=0))
    kk = kk * lax.rsqrt(jnp.maximum(sums[:tb], 1e-24))
    bonus_all = sums[tb:] * v_all
    a_all = -kk
    b_all = kk * alr

    chunks = range(RWKV_NCH)
    rows = [slice(ci * c, (ci + 1) * c) for ci in chunks]
    cl = [_dot_split2_lhs(tri_ref[...], lw_all[rw]) for rw in rows]
    pre = []
    for ci in chunks:
        rw = rows[ci]
        cl_last = cl[ci][c - 1:c, :]
        e_neg = jnp.exp(-cl[ci])
        e_end = jnp.exp(cl_last - cl[ci])
        b, k2, v = b_all[rw], k2_all[rw], v_all[rw]
        at = a_all[rw] * jnp.exp(cl[ci] - lw_all[rw])
        rt = r_all[rw] * jnp.exp(cl[ci])
        pre.append(dict(
            rt=rt, v=v, at=at, g_c=jnp.exp(cl_last),
            lhs=jnp.concatenate([at, rt], axis=0).astype(BF16),
            rhs=jnp.concatenate([block_diag(b * e_neg), block_diag(k2 * e_neg)], axis=0),
            hat=jnp.concatenate([b * e_end, k2 * e_end], axis=0).astype(BF16)))
    for p in pre:
        aa = _dot_nt(p["lhs"], p["rhs"])
        p.update(
            a_ab=jnp.where(strict, aa[:c, :gw], 0.0),
            a_ak=jnp.where(strict, aa[:c, gw:], 0.0).astype(BF16),
            p_rb=jnp.where(incl, aa[c:, :gw], 0.0).astype(BF16),
            p_rk=jnp.where(incl, aa[c:, gw:], 0.0).astype(BF16))

    a_bd = [block_diag(p["a_ab"]) for p in pre]
    tinv = [jnp.where(tt == jj, 1.0, jnp.where((tt >> 1) == (jj >> 1), p["a_ab"], 0.0)) for p in pre]
    for s in RWKV_INV_LEVELS:
        shift = int(np.log2(2 * s))
        off = ((tt >> shift) == (jj >> shift)) & ((tt & s) != 0) & ((jj & s) == 0)
        t_bd = [block_diag(t) for t in tinv]
        half = [jnp.where(off, _dot(tinv[ci].astype(BF16), a_bd[ci]), 0.0).astype(BF16) for ci in chunks]
        tinv = [tinv[ci] + _dot(half[ci], t_bd[ci]) for ci in chunks]

    t_b = [t.astype(BF16) for t in tinv]
    v_bd = [block_diag(p["v"]) for p in pre]
    wt = [_dot(t_b[ci], block_diag(pre[ci]["at"])) for ci in chunks]
    akv = [block_diag(_dot(pre[ci]["a_ak"], v_bd[ci])) for ci in chunks]
    u0 = [_dot(t_b[ci], akv[ci]) for ci in chunks]
    qh = [(pre[ci]["rt"] + _dot(pre[ci]["p_rb"], block_diag(wt[ci]))).astype(BF16) for ci in chunks]
    gmat = [jnp.where(same_head, _dot(wt[ci].T.astype(BF16), pre[ci]["hat"][:c]), 0.0).astype(BF16)
            for ci in chunks]
    y0 = [_dot(pre[ci]["p_rb"], block_diag(u0[ci])) + _dot(pre[ci]["p_rk"], v_bd[ci]) for ci in chunks]
    n0c = []
    for ci in chunks:
        uv_t = jnp.concatenate([u0[ci], pre[ci]["v"]], axis=0).T.astype(BF16)
        n0 = jnp.where(same_head, _dot(uv_t, pre[ci]["hat"]), 0.0)
        n0c.append(n0[0:c] + n0[c:2 * c] + n0[2 * c:3 * c] + n0[3 * c:4 * c])

    st = st_scr[...]
    ys = []
    for ci in chunks:
        ys.append(_dot_nt(qh[ci], block_diag(st)) + y0[ci])
        st = st * pre[ci]["g_c"] + _dot(st.astype(BF16), gmat[ci]) + n0c[ci]
    st_scr[...] = st

    y = jnp.concatenate(ys, axis=0)
    yc = y - seg_sum(y) * inv_n
    var = seg_sum(yc * yc) * inv_n
    yn = yc * lax.rsqrt(var + RWKV_GN_EPS) * gnw_ref[...] + gnb_ref[...]
    o_ref[...] = ((yn + bonus_all) * gate_all).astype(o_ref.dtype)


def _rwkv_call(z, z_first, vec, mats, tri, ones_bd, has_vres):
    t = z.shape[0]
    tb = CHUNK * RWKV_NCH
    gw = RWKV_GW
    groups = RWKV_HEADS // RWKV_GROUP
    col = lambda off: pl.BlockSpec((tb, gw), lambda p, c: (c, off // gw + p))
    fixed = lambda off, w: pl.BlockSpec((tb, w), lambda p, c: (c, off // w))
    rowv = pl.BlockSpec((1, gw), lambda p, c: (0, p))
    upm = lambda rows: pl.BlockSpec((rows, gw), lambda p, c: (0, p))
    const = lambda a: pl.BlockSpec(a.shape, lambda p, c: (0, 0))

    in_specs = [col(C_RR), col(C_RK), col(C_RV), fixed(C_RW, LANES), fixed(C_RA, LANES), fixed(C_RG, 2 * LANES)]
    args = [z, z, z, z, z, z]
    if has_vres:
        in_specs += [fixed(C_VRES, LANES), col(C_RV), upm(LANES), rowv]
        args += [z, z_first, mats["v_up"], vec["v0"]]
    in_specs += [rowv, upm(LANES), rowv, upm(LANES), upm(2 * LANES), rowv, rowv, rowv, rowv, rowv,
                 const(tri), const(ones_bd)]
    args += [vec["w0"], mats["w_up"], vec["a0"], mats["a_up"], mats["g_up"], vec["k_k"], vec["k_a"],
             vec["r_k"], vec["gn_w"], vec["gn_b"], tri, ones_bd]
    return pl.pallas_call(
        functools.partial(_rwkv_kernel, has_vres=has_vres),
        out_shape=jax.ShapeDtypeStruct((t, RWKV_WIDTH), BF16),
        grid=(groups, t // tb),
        in_specs=in_specs,
        out_specs=pl.BlockSpec((tb, gw), lambda p, c: (c, p)),
        scratch_shapes=[pltpu.VMEM((CHUNK, RWKV_GW), F32)],
        compiler_params=pltpu.CompilerParams(
            dimension_semantics=("arbitrary", "arbitrary"), vmem_limit_bytes=VMEM_LIMIT_BYTES),
        name="rwkv7_mixer",
    )(*args)


OUT_TM = 256


def _outproj_kernel(x_ref, oa_ref, ob_ref, wa_ref, wb_ref, g_ref, gt_ref, o_ref):
    y = _dot(oa_ref[...], wa_ref[...]) + _dot(ob_ref[...], wb_ref[...])
    ms = jnp.mean(y * y, axis=-1, keepdims=True)
    o_ref[...] = x_ref[...] + gt_ref[0] * (y * lax.rsqrt(ms + RMS_EPS) * g_ref[...])


def _outproj_call(x2, o_gla, o_rwkv, w_a, w_b, g_row, mod, layer):
    t, d = x2.shape
    return pl.pallas_call(
        _outproj_kernel,
        out_shape=jax.ShapeDtypeStruct((t, d), F32),
        grid=(t // OUT_TM,),
        in_specs=[
            pl.BlockSpec((OUT_TM, d), lambda i: (i, 0)),
            pl.BlockSpec((OUT_TM, GLA_V), lambda i: (i, 0)),
            pl.BlockSpec((OUT_TM, RWKV_WIDTH), lambda i: (i, 0)),
            pl.BlockSpec((GLA_V, d), lambda i: (0, 0)),
            pl.BlockSpec((RWKV_WIDTH, d), lambda i: (0, 0)),
            pl.BlockSpec((1, d), lambda i: (0, 0)),
            pl.BlockSpec((1, 1, d), lambda i: (layer, 0, 2)),
        ],
        out_specs=pl.BlockSpec((OUT_TM, d), lambda i: (i, 0)),
        compiler_params=pltpu.CompilerParams(
            dimension_semantics=("arbitrary",), vmem_limit_bytes=VMEM_LIMIT_BYTES),
        name="outproj",
    )(x2, o_gla, o_rwkv, w_a, w_b, g_row, mod)


FFN_TM = 512
FFN_TF = 1024


def _ffn_kernel(x_ref, gpre_ref, sh_ref, sc_ref, w1_ref, w2_ref, gpost_ref, gt_ref, o_ref, h_scr, acc_scr):
    f = pl.program_id(1)

    @pl.when(f == 0)
    def _():
        x = x_ref[...]
        ms = jnp.mean(x * x, axis=-1, keepdims=True)
        y = x * lax.rsqrt(ms + RMS_EPS) * gpre_ref[...]
        h_scr[...] = (y * (1.0 + sc_ref[0]) + sh_ref[0]).astype(BF16)
        acc_scr[...] = jnp.zeros(acc_scr.shape, F32)

    u = jnp.maximum(_dot(h_scr[...], w1_ref[...]), 0.0)
    acc_scr[...] += _dot((u * u).astype(BF16), w2_ref[...])

    @pl.when(f == pl.num_programs(1) - 1)
    def _():
        y = acc_scr[...]
        ms = jnp.mean(y * y, axis=-1, keepdims=True)
        o_ref[...] = x_ref[...] + gt_ref[0] * (y * lax.rsqrt(ms + RMS_EPS) * gpost_ref[...])


def _ffn_call(x2, gpre_row, gpost_row, mod, layer, w1, w2):
    t, d = x2.shape
    dff = w1.shape[1]
    return pl.pallas_call(
        _ffn_kernel,
        out_shape=jax.ShapeDtypeStruct((t, d), F32),
        grid=(t // FFN_TM, dff // FFN_TF),
        in_specs=[
            pl.BlockSpec((FFN_TM, d), lambda i, f: (i, 0)),
            pl.BlockSpec((1, d), lambda i, f: (0, 0)),
            pl.BlockSpec((1, 1, d), lambda i, f: (layer, 0, 3)),
            pl.BlockSpec((1, 1, d), lambda i, f: (layer, 0, 4)),
            pl.BlockSpec((d, FFN_TF), lambda i, f: (0, f)),
            pl.BlockSpec((FFN_TF, d), lambda i, f: (f, 0)),
            pl.BlockSpec((1, d), lambda i, f: (0, 0)),
            pl.BlockSpec((1, 1, d), lambda i, f: (layer, 0, 5)),
        ],
        out_specs=pl.BlockSpec((FFN_TM, d), lambda i, f: (i, 0)),
        scratch_shapes=[pltpu.VMEM((FFN_TM, d), BF16), pltpu.VMEM((FFN_TM, d), F32)],
        compiler_params=pltpu.CompilerParams(
            dimension_semantics=("arbitrary", "arbitrary"), vmem_limit_bytes=VMEM_LIMIT_BYTES),
        name="ffn",
    )(x2, gpre_row, mod, mod, w1, w2, gpost_row, mod)


def _pad_cols(w, width):
    return jnp.pad(w, ((0, 0), (0, width - w.shape[1])))


def _pad_rows(w, rows):
    return jnp.pad(w, ((0, rows - w.shape[0]), (0, 0)))


def _pack_inproj(w_in, mu_rwkv, vres_w_down, vres_mu):
    d = w_in.shape[0]
    gq, gk, gv, gg, ga, rr, rk, rv, rw, ra, rg = jnp.split(
        w_in, np.cumsum([512, 512, 1024, 1024, 16, 1024, 1024, 1024, 96, 96]).tolist(), axis=1)
    mr, mk, mv, mw, ma, mg = jnp.split(mu_rwkv[None, :], np.cumsum([1024, 1024, 1024, 96, 96]).tolist(), axis=1)
    if vres_w_down is None:
        vres_w_down = jnp.zeros((d, RWKV_V_RANK), w_in.dtype)
        vres_mu = jnp.zeros((RWKV_V_RANK,), w_in.dtype)
    w_pack = jnp.concatenate(
        [gq, gk, gv, gg, _pad_cols(ga, LANES), _pad_cols(vres_w_down, LANES),
         rr, rk, rv, rg, _pad_cols(rw, LANES), _pad_cols(ra, LANES)], axis=1).astype(BF16)
    mu_pack = jnp.concatenate(
        [jnp.zeros((1, C_VRES), F32), _pad_cols(vres_mu[None, :], LANES),
         mr, mk, mv, mg, _pad_cols(mw, LANES), _pad_cols(ma, LANES)], axis=1)
    return w_pack, mu_pack


def kernel(x, c, w_ada, b_ada, g_pre_mix, g_post_mix, g_pre_ffn, g_post_ffn, w_in, gla_w_a_up, gla_b_a, gla_norm_w, rwkv_mu, rwkv_w0, rwkv_w_up, rwkv_a0, rwkv_a_up, rwkv_g_up, rwkv_k_k, rwkv_k_a, rwkv_r_k, rwkv_gn_w, rwkv_gn_b, vres_w_down, vres_mu, vres_up, vres_v0, w_out, w_ff1, w_ff2):
    bsz, t, d = x.shape
    assert bsz == 1 and d == D_MODEL and t % (CHUNK * max(GLA_NCH, RWKV_NCH)) == 0 and t % FFN_TM == 0
    n_layers = w_ada.shape[0]

    mod = _ada_call(c.reshape(d, 1), w_ada, b_ada.reshape(n_layers, 1, 6 * d))

    p_gla = jnp.asarray(_gla_exponent_matrix(), BF16)
    tri = jnp.asarray(np.tril(np.ones((CHUNK, CHUNK), np.float32)), BF16)
    head_of_lane = np.arange(RWKV_GW) // RWKV_HEAD
    ones_bd = jnp.asarray((head_of_lane[:, None] == head_of_lane[None, :]).astype(np.float32), BF16)

    x2 = x.reshape(t, d)
    z_first = None
    for i in range(n_layers):
        j = i - 1
        w_pack, mu_pack = _pack_inproj(
            w_in[i], rwkv_mu[i],
            vres_w_down[j] if i > 0 else None, vres_mu[j] if i > 0 else None)
        z = _inproj_call(x2, g_pre_mix[i][None, :], mod, i, w_pack, mu_pack)
        if i == 0:
            z_first = z

        o_gla = _gla_call(
            z, _pad_rows(gla_w_a_up[i], LANES).astype(BF16), gla_b_a[i][None, :], gla_norm_w[i][None, :], p_gla)

        row = lambda a: a.reshape(1, RWKV_WIDTH)
        vec = dict(w0=row(rwkv_w0[i]), a0=row(rwkv_a0[i]), k_k=row(rwkv_k_k[i]), k_a=row(rwkv_k_a[i]),
                   r_k=row(rwkv_r_k[i]), gn_w=row(rwkv_gn_w[i]), gn_b=row(rwkv_gn_b[i]))
        mats = dict(w_up=_pad_rows(rwkv_w_up[i], LANES).astype(BF16),
                    a_up=_pad_rows(rwkv_a_up[i], LANES).astype(BF16),
                    g_up=rwkv_g_up[i].astype(BF16))
        if i > 0:
            vec["v0"] = row(vres_v0[j])
            mats["v_up"] = _pad_rows(vres_up[j], LANES).astype(BF16)
        o_rwkv = _rwkv_call(z, z_first, vec, mats, tri, ones_bd, has_vres=i > 0)

        w_o = w_out[i].astype(BF16)
        x2 = _outproj_call(x2, o_gla, o_rwkv, w_o[:GLA_V], w_o[GLA_V:], g_post_mix[i][None, :], mod, i)
        x2 = _ffn_call(x2, g_pre_ffn[i][None, :], g_post_ffn[i][None, :], mod, i,
                       w_ff1[i].astype(BF16), w_ff2[i].astype(BF16))
    return x2.reshape(bsz, t, d)
```

```python
import functools

import numpy as np
import jax
import jax.numpy as jnp
from jax import lax
from jax.experimental import pallas as pl
from jax.experimental.pallas import tpu as pltpu

F32 = jnp.float32
BF16 = jnp.bfloat16

D_MODEL = 2048
DEPTH = 2
GLA_V = 1024
GLA_DV = 128
GLA_HEADS = 8
GLA_DK = 64
GLA_QK = 512
GLA_GATE_RANK = 16
GLA_TAU = 16.0
RWKV_WIDTH = 1024
RWKV_HEAD = 64
RWKV_HEADS = 16
RWKV_W_RANK = 96
RWKV_A_RANK = 96
RWKV_G_RANK = 256
RWKV_V_RANK = 64
RWKV_GN_EPS = 64e-5
D_FF = 4 * D_MODEL
RMS_EPS = 1e-6

LANES = 128
CHUNK = 64
VMEM_LIMIT_BYTES = 56 * 1024 * 1024

C_GQ = 0
C_GK = 512
C_GV = 1024
C_GG = 2048
C_GA = 3072
C_VRES = 3200
C_RR = 3328
C_RK = 4352
C_RV = 5376
C_RG = 6400
C_RW = 6656
C_RA = 6784
N_PACK = 6912


def _dot(a, b):
    return jnp.dot(a, b, preferred_element_type=F32)


def _dot_nt(a, b):
    return lax.dot_general(a, b, (((1,), (1,)), ((), ())), preferred_element_type=F32)


def _dot_split2_lhs(p, x):
    hi = x.astype(BF16)
    lo = (x - hi.astype(F32)).astype(BF16)
    y = _dot(p, jnp.concatenate([hi, lo], axis=1))
    return y[:, :x.shape[1]] + y[:, x.shape[1]:]


def _dot_split2_rhs(x, p):
    hi = x.astype(BF16)
    lo = (x - hi.astype(F32)).astype(BF16)
    y = _dot(jnp.concatenate([hi, lo], axis=0), p)
    return y[:x.shape[0]] + y[x.shape[0]:]


def _log_sigmoid(x):
    return jnp.minimum(x, 0.0) - jnp.log(1.0 + jnp.exp(-jnp.abs(x)))


def _sigmoid(x):
    return 1.0 / (1.0 + jnp.exp(-x))


ADA_TN = 1024


def _ada_kernel(c_ref, w_ref, b_ref, o_ref):
    c = c_ref[...]
    cond = c * _sigmoid(c)
    o_ref[0] = jnp.sum(w_ref[0] * cond, axis=0, keepdims=True) + b_ref[0]


def _ada_call(c_col, w_ada, b_ada3):
    n_layers, d, n = w_ada.shape
    return pl.pallas_call(
        _ada_kernel,
        out_shape=jax.ShapeDtypeStruct((n_layers, 1, n), F32),
        grid=(n_layers, n // ADA_TN),
        in_specs=[
            pl.BlockSpec((d, 1), lambda l, j: (0, 0)),
            pl.BlockSpec((1, d, ADA_TN), lambda l, j: (l, 0, j)),
            pl.BlockSpec((1, 1, ADA_TN), lambda l, j: (l, 0, j)),
        ],
        out_specs=pl.BlockSpec((1, 1, ADA_TN), lambda l, j: (l, 0, j)),
        compiler_params=pltpu.CompilerParams(
            dimension_semantics=("arbitrary", "arbitrary"), vmem_limit_bytes=VMEM_LIMIT_BYTES),
        name="ada_mod",
    )(c_col, w_ada, b_ada3)


INP_TM = 1024
INP_TN = 768


def _inproj_kernel(x_ref, g_ref, sh_ref, sc_ref, w_ref, mu_ref, z_ref, h_scr, carry_scr):
    i = pl.program_id(0)
    j = pl.program_id(1)

    @pl.when(j == 0)
    def _():
        x = x_ref[...]
        ms = jnp.mean(x * x, axis=-1, keepdims=True)
        y = x * lax.rsqrt(ms + RMS_EPS) * g_ref[...]
        h_scr[...] = (y * (1.0 + sc_ref[0]) + sh_ref[0]).astype(BF16)

    @pl.when(i == 0)
    def _():
        carry_scr[j] = jnp.zeros(carry_scr.shape[1:], F32)

    z = _dot(h_scr[...], w_ref[...])
    prev = carry_scr[j]
    row = lax.broadcasted_iota(jnp.int32, z.shape, 0)
    shifted = jnp.where(row == 0, prev, pltpu.roll(z, 1, 0))
    carry_scr[j] = z[z.shape[0] - 1:, :]
    z_ref[...] = (z + mu_ref[...] * (shifted - z)).astype(z_ref.dtype)


def _inproj_call(x2, g_row, mod, layer, w_pack, mu_pack):
    t, d = x2.shape
    n = w_pack.shape[1]
    nj = n // INP_TN
    return pl.pallas_call(
        _inproj_kernel,
        out_shape=jax.ShapeDtypeStruct((t, n), BF16),
        grid=(t // INP_TM, nj),
        in_specs=[
            pl.BlockSpec((INP_TM, d), lambda i, j: (i, 0)),
            pl.BlockSpec((1, d), lambda i, j: (0, 0)),
            pl.BlockSpec((1, 1, d), lambda i, j: (layer, 0, 0)),
            pl.BlockSpec((1, 1, d), lambda i, j: (layer, 0, 1)),
            pl.BlockSpec((d, INP_TN), lambda i, j: (0, j)),
            pl.BlockSpec((1, INP_TN), lambda i, j: (0, j)),
        ],
        out_specs=pl.BlockSpec((INP_TM, INP_TN), lambda i, j: (i, j)),
        scratch_shapes=[pltpu.VMEM((INP_TM, d), BF16), pltpu.VMEM((nj, 1, INP_TN), F32)],
        compiler_params=pltpu.CompilerParams(
            dimension_semantics=("arbitrary", "arbitrary"), vmem_limit_bytes=VMEM_LIMIT_BYTES),
        name="inproj",
    )(x2, g_row, mod, mod, w_pack, mu_pack)


GLA_NCH = 8
GLA_GROUP = 4
GLA_QW = GLA_GROUP * GLA_DK
GLA_VW = GLA_GROUP * GLA_DV
GLA_LEVELS = (32, 16, 8, 4, 2, 1)


def _gla_exponent_matrix():
    c = CHUNK
    p = np.zeros((2 * c + len(GLA_LEVELS) * c, c), np.float32)
    for i in range(c):
        p[i, : i + 1] = 1.0
        p[c + i, i + 1:] = 1.0
    for li, s in enumerate(GLA_LEVELS):
        base = 2 * c + li * c
        for i in range(c):
            m = (i // (2 * s)) * (2 * s) + s
            if i & s:
                p[base + i, m + 1: i + 1] = 1.0
            else:
                p[base + i, i + 1: m + 1] = 1.0
    return p


def _gla_masks():
    hq = np.arange(GLA_QW) // GLA_DK
    hv = np.arange(GLA_VW) // GLA_DV
    qq = (hq[:, None] == hq[None, :]).astype(np.float32)
    qv = (hq[:, None] == hv[None, :]).astype(np.float32)
    return qq, qv, qv.T.copy()


def _gla_kernel(q_ref, k_ref, v_ref, g_ref, al_ref, wup_ref, ba_ref, nw_ref, p_ref, mqq_ref, mqv_ref, mvq_ref,
                o_ref, st_scr):
    @pl.when(pl.program_id(1) == 0)
    def _():
        st_scr[...] = jnp.zeros(st_scr.shape, F32)

    c = CHUNK
    qw, vw = GLA_QW, GLA_VW
    tt = lax.broadcasted_iota(jnp.int32, (c, qw), 0)
    jj = lax.broadcasted_iota(jnp.int32, (c, qw), 1) & (GLA_DK - 1)
    mqq, mqv, mvq = mqq_ref[...], mqv_ref[...], mvq_ref[...]
    same_head_vq = ((lax.broadcasted_iota(jnp.int32, (vw, qw), 0) >> int(np.log2(GLA_DV)))
                    == (lax.broadcasted_iota(jnp.int32, (vw, qw), 1) >> int(np.log2(GLA_DK))))

    def bd_qk(x):
        return jnp.concatenate([x.astype(BF16)] * GLA_GROUP, axis=0) * mqq

    x = _dot(al_ref[...].astype(BF16), wup_ref[...]) + ba_ref[...]
    la_all = _log_sigmoid(x) * (1.0 / GLA_TAU)
    q_all = q_ref[...].astype(F32) * (GLA_DK ** -0.5)
    k_all = k_ref[...].astype(F32)
    v_all = v_ref[...].astype(F32)

    chunks = range(GLA_NCH)
    rows = [slice(ci * c, (ci + 1) * c) for ci in chunks]
    e_all = [_dot_split2_lhs(p_ref[...], la_all[rw]) for rw in rows]
    q = [q_all[rw] for rw in rows]
    k = [k_all[rw] for rw in rows]

    scores = [jnp.where(tt == jj, _dot_nt(q[ci].astype(BF16), bd_qk(k[ci])), 0.0) for ci in chunks]
    for li, s in enumerate(GLA_LEVELS):
        second = (tt & s) != 0
        shift = int(np.log2(2 * s))
        same = (tt >> shift) == (jj >> shift)
        e = [jnp.exp(e_all[ci][(2 + li) * c:(3 + li) * c]) for ci in chunks]
        qd = [jnp.where(second, q[ci] * e[ci], 0.0).astype(BF16) for ci in chunks]
        kd = [bd_qk(jnp.where(second, 0.0, k[ci] * e[ci])) for ci in chunks]
        scores = [scores[ci] + jnp.where(same, _dot_nt(qd[ci], kd[ci]), 0.0) for ci in chunks]

    v_bd = [jnp.concatenate([v_all[rw].astype(BF16)] * GLA_GROUP, axis=0) * mqv for rw in rows]
    o_intra = [_dot(scores[ci].astype(BF16), v_bd[ci]) for ci in chunks]
    upd = []
    for ci in chunks:
        ke = (k[ci] * jnp.exp(e_all[ci][c:2 * c])).astype(BF16)
        m = jnp.where(same_head_vq, _dot(v_all[rows[ci]].T.astype(BF16), ke), 0.0)
        dv = GLA_DV
        upd.append(m[0:dv] + m[dv:2 * dv] + m[2 * dv:3 * dv] + m[3 * dv:4 * dv])

    st = st_scr[...]
    os_ = []
    for ci in chunks:
        b = e_all[ci][0:c]
        st_bd = jnp.concatenate([st.astype(BF16)] * GLA_GROUP, axis=0) * mvq
        os_.append(_dot_nt((q[ci] * jnp.exp(b)).astype(BF16), st_bd) + o_intra[ci])
        st = st * jnp.exp(b[c - 1:c, :]) + upd[ci]
    st_scr[...] = st

    o = jnp.concatenate(os_, axis=0)
    g = g_ref[...].astype(F32)
    gs = g * _sigmoid(g) * nw_ref[...]
    outs = []
    for h in range(GLA_GROUP):
        oh = o[:, h * GLA_DV:(h + 1) * GLA_DV]
        oh = oh * lax.rsqrt(jnp.mean(oh * oh, axis=-1, keepdims=True) + RMS_EPS)
        outs.append(oh * gs[:, h * GLA_DV:(h + 1) * GLA_DV])
    o_ref[...] = jnp.concatenate(outs, axis=1).astype(o_ref.dtype)


def _gla_call(z, wup_pad, ba_row, nw_row, p_mat, masks):
    t = z.shape[0]
    tb = CHUNK * GLA_NCH
    groups = GLA_HEADS // GLA_GROUP
    blk = lambda w, off: pl.BlockSpec((tb, w), lambda p, c: (c, off // w + p))
    const = lambda a: pl.BlockSpec(a.shape, lambda p, c: (0, 0))
    return pl.pallas_call(
        _gla_kernel,
        out_shape=jax.ShapeDtypeStruct((t, GLA_V), BF16),
        grid=(groups, t // tb),
        in_specs=[
            blk(GLA_QW, C_GQ), blk(GLA_QW, C_GK), blk(GLA_VW, C_GV), blk(GLA_VW, C_GG),
            pl.BlockSpec((tb, LANES), lambda p, c: (c, C_GA // LANES)),
            pl.BlockSpec((LANES, GLA_QW), lambda p, c: (0, p)),
            pl.BlockSpec((1, GLA_QW), lambda p, c: (0, p)),
            pl.BlockSpec((1, GLA_VW), lambda p, c: (0, p)),
            const(p_mat), const(masks[0]), const(masks[1]), const(masks[2]),
        ],
        out_specs=pl.BlockSpec((tb, GLA_VW), lambda p, c: (c, p)),
        scratch_shapes=[pltpu.VMEM((GLA_DV, GLA_QW), F32)],
        compiler_params=pltpu.CompilerParams(
            dimension_semantics=("arbitrary", "arbitrary"), vmem_limit_bytes=VMEM_LIMIT_BYTES),
        name="gla_mixer",
    )(z, z, z, z, z, wup_pad, ba_row, nw_row, p_mat, *masks)


RWKV_NCH = 8
RWKV_GROUP = 4
RWKV_GW = RWKV_GROUP * RWKV_HEAD
RWKV_INV_LEVELS = (2, 4, 8, 16, 32)


def _rwkv_kernel(*refs, has_vres):
    if has_vres:
        (r_ref, k_ref, v_ref, wl_ref, al_ref, gl_ref, vl_ref, vf_ref, vup_ref, v0_ref,
         w0_ref, wup_ref, a0_ref, aup_ref, gup_ref, kk_ref, ka_ref, rk_ref, gnw_ref, gnb_ref,
         tri_ref, ones_ref, o_ref, st_scr) = refs
    else:
        (r_ref, k_ref, v_ref, wl_ref, al_ref, gl_ref,
         w0_ref, wup_ref, a0_ref, aup_ref, gup_ref, kk_ref, ka_ref, rk_ref, gnw_ref, gnb_ref,
         tri_ref, ones_ref, o_ref, st_scr) = refs

    @pl.when(pl.program_id(1) == 0)
    def _():
        st_scr[...] = jnp.zeros(st_scr.shape, F32)

    c = CHUNK
    gw = RWKV_GW
    tt = lax.broadcasted_iota(jnp.int32, (c, gw), 0)
    jj = lax.broadcasted_iota(jnp.int32, (c, gw), 1) & (RWKV_HEAD - 1)
    strict = tt > jj
    incl = tt >= jj
    ones_bd = ones_ref[...]
    same_head_b = ones_bd != 0
    hshift = int(np.log2(RWKV_HEAD))
    same_head = ((lax.broadcasted_iota(jnp.int32, (gw, gw), 0) >> hshift)
                 == (lax.broadcasted_iota(jnp.int32, (gw, gw), 1) >> hshift))
    inv_n = 1.0 / RWKV_HEAD

    def seg_sum(x):
        return _dot_split2_rhs(x, ones_bd)

    def block_diag(x):
        return jnp.concatenate([x.astype(BF16)] * RWKV_GROUP, axis=0) * ones_bd

    tb = c * RWKV_NCH
    r_all = r_ref[...].astype(F32)
    k_all = k_ref[...].astype(F32)
    v_all = v_ref[...].astype(F32)
    w_pre = w0_ref[...] + _dot(jnp.tanh(wl_ref[...].astype(F32)).astype(BF16), wup_ref[...])
    lw_all = -jnp.exp(_log_sigmoid(w_pre) - 0.5)
    alr = _sigmoid(a0_ref[...] + _dot(al_ref[...].astype(BF16), aup_ref[...]))
    gate_all = _dot(_sigmoid(gl_ref[...].astype(F32)).astype(BF16), gup_ref[...])
    if has_vres:
        mix = _sigmoid(v0_ref[...] + _dot(vl_ref[...].astype(BF16), vup_ref[...]))
        v_all = v_all + (vf_ref[...].astype(F32) - v_all) * mix
    kk = k_all * kk_ref[...]
    k2_all = k_all * (1.0 + (alr - 1.0) * ka_ref[...])
    sums = seg_sum(jnp.concatenate([kk * kk, r_all * k2_all * rk_ref[...]], axis=0))
    kk = kk * lax.rsqrt(jnp.maximum(sums[:tb], 1e-24))
    bonus_all = sums[tb:] * v_all
    a_all = -kk
    b_all = kk * alr

    chunks = range(RWKV_NCH)
    rows = [slice(ci * c, (ci + 1) * c) for ci in chunks]
    cl = [_dot_split2_lhs(tri_ref[...], lw_all[rw]) for rw in rows]
    pre = []
    for ci in chunks:
        rw = rows[ci]
        cl_last = cl[ci][c - 1:c, :]
        e_neg = jnp.exp(-cl[ci])
        e_end = jnp.exp(cl_last - cl[ci])
        b, k2, v = b_all[rw], k2_all[rw], v_all[rw]
        at = a_all[rw] * jnp.exp(cl[ci] - lw_all[rw])
        rt = r_all[rw] * jnp.exp(cl[ci])
        pre.append(dict(
            rt=rt, v=v, at=at, g_c=jnp.exp(cl_last),
            lhs=jnp.concatenate([at, rt], axis=0).astype(BF16),
            rhs=jnp.concatenate([block_diag(b * e_neg), block_diag(k2 * e_neg)], axis=0),
            hat=jnp.concatenate([b * e_end, k2 * e_end], axis=0).astype(BF16)))
    for p in pre:
        aa = _dot_nt(p["lhs"], p["rhs"])
        p.update(
            a_ab=jnp.where(strict, aa[:c, :gw], 0.0),
            a_ak=jnp.where(strict, aa[:c, gw:], 0.0).astype(BF16),
            p_rb=jnp.where(incl, aa[c:, :gw], 0.0).astype(BF16),
            p_rk=jnp.where(incl, aa[c:, gw:], 0.0).astype(BF16))

    a_bd = [block_diag(p["a_ab"]) for p in pre]
    tinv = [jnp.where(tt == jj, 1.0, jnp.where((tt >> 1) == (jj >> 1), p["a_ab"], 0.0)) for p in pre]
    for s in RWKV_INV_LEVELS:
        shift = int(np.log2(2 * s))
        off = ((tt >> shift) == (jj >> shift)) & ((tt & s) != 0) & ((jj & s) == 0)
        t_bd = [block_diag(t) for t in tinv]
        half = [jnp.where(off, _dot(tinv[ci].astype(BF16), a_bd[ci]), 0.0).astype(BF16) for ci in chunks]
        tinv = [tinv[ci] + _dot(half[ci], t_bd[ci]) for ci in chunks]

    t_b = [t.astype(BF16) for t in tinv]
    v_bd = [block_diag(p["v"]) for p in pre]
    wt = [_dot(t_b[ci], block_diag(pre[ci]["at"])) for ci in chunks]
    akv = [block_diag(_dot(pre[ci]["a_ak"], v_bd[ci])) for ci in chunks]
    u0 = [_dot(t_b[ci], akv[ci]) for ci in chunks]
    qh = [(pre[ci]["rt"] + _dot(pre[ci]["p_rb"], block_diag(wt[ci]))).astype(BF16) for ci in chunks]
    gmat = [jnp.where(same_head, _dot(wt[ci].T.astype(BF16), pre[ci]["hat"][:c]), 0.0).astype(BF16)
            for ci in chunks]
    y0 = [_dot(pre[ci]["p_rb"], block_diag(u0[ci])) + _dot(pre[ci]["p_rk"], v_bd[ci]) for ci in chunks]
    n0c = []
    for ci in chunks:
        uv_t = jnp.concatenate([u0[ci], pre[ci]["v"]], axis=0).T.astype(BF16)
        n0 = jnp.where(same_head, _dot(uv_t, pre[ci]["hat"]), 0.0)
        n0c.append(n0[0:c] + n0[c:2 * c] + n0[2 * c:3 * c] + n0[3 * c:4 * c])

    st = st_scr[...]
    ys = []
    for ci in chunks:
        ys.append(_dot_nt(qh[ci], block_diag(st)) + y0[ci])
        st = st * pre[ci]["g_c"] + _dot(st.astype(BF16), gmat[ci]) + n0c[ci]
    st_scr[...] = st

    y = jnp.concatenate(ys, axis=0)
    yc = y - seg_sum(y) * inv_n
    var = seg_sum(yc * yc) * inv_n
    yn = yc * lax.rsqrt(var + RWKV_GN_EPS) * gnw_ref[...] + gnb_ref[...]
    o_ref[...] = ((yn + bonus_all) * gate_all).astype(o_ref.dtype)


def _rwkv_call(z, z_first, vec, mats, tri, ones_bd, has_vres):
    t = z.shape[0]
    tb = CHUNK * RWKV_NCH
    gw = RWKV_GW
    groups = RWKV_HEADS // RWKV_GROUP
    col = lambda off: pl.BlockSpec((tb, gw), lambda p, c: (c, off // gw + p))
    fixed = lambda off, w: pl.BlockSpec((tb, w), lambda p, c: (c, off // w))
    rowv = pl.BlockSpec((1, gw), lambda p, c: (0, p))
    upm = lambda rows: pl.BlockSpec((rows, gw), lambda p, c: (0, p))
    const = lambda a: pl.BlockSpec(a.shape, lambda p, c: (0, 0))

    in_specs = [col(C_RR), col(C_RK), col(C_RV), fixed(C_RW, LANES), fixed(C_RA, LANES), fixed(C_RG, 2 * LANES)]
    args = [z, z, z, z, z, z]
    if has_vres:
        in_specs += [fixed(C_VRES, LANES), col(C_RV), upm(LANES), rowv]
        args += [z, z_first, mats["v_up"], vec["v0"]]
    in_specs += [rowv, upm(LANES), rowv, upm(LANES), upm(2 * LANES), rowv, rowv, rowv, rowv, rowv,
                 const(tri), const(ones_bd)]
    args += [vec["w0"], mats["w_up"], vec["a0"], mats["a_up"], mats["g_up"], vec["k_k"], vec["k_a"],
             vec["r_k"], vec["gn_w"], vec["gn_b"], tri, ones_bd]
    return pl.pallas_call(
        functools.partial(_rwkv_kernel, has_vres=has_vres),
        out_shape=jax.ShapeDtypeStruct((t, RWKV_WIDTH), BF16),
        grid=(groups, t // tb),
        in_specs=in_specs,
        out_specs=pl.BlockSpec((tb, gw), lambda p, c: (c, p)),
        scratch_shapes=[pltpu.VMEM((CHUNK, RWKV_GW), F32)],
        compiler_params=pltpu.CompilerParams(
            dimension_semantics=("arbitrary", "arbitrary"), vmem_limit_bytes=VMEM_LIMIT_BYTES),
        name="rwkv7_mixer",
    )(*args)


OUT_TM = 256


def _outproj_kernel(x_ref, oa_ref, ob_ref, wa_ref, wb_ref, g_ref, gt_ref, o_ref):
    y = _dot(oa_ref[...], wa_ref[...]) + _dot(ob_ref[...], wb_ref[...])
    ms = jnp.mean(y * y, axis=-1, keepdims=True)
    o_ref[...] = x_ref[...] + gt_ref[0] * (y * lax.rsqrt(ms + RMS_EPS) * g_ref[...])


def _outproj_call(x2, o_gla, o_rwkv, w_a, w_b, g_row, mod, layer):
    t, d = x2.shape
    return pl.pallas_call(
        _outproj_kernel,
        out_shape=jax.ShapeDtypeStruct((t, d), F32),
        grid=(t // OUT_TM,),
        in_specs=[
            pl.BlockSpec((OUT_TM, d), lambda i: (i, 0)),
            pl.BlockSpec((OUT_TM, GLA_V), lambda i: (i, 0)),
            pl.BlockSpec((OUT_TM, RWKV_WIDTH), lambda i: (i, 0)),
            pl.BlockSpec((GLA_V, d), lambda i: (0, 0)),
            pl.BlockSpec((RWKV_WIDTH, d), lambda i: (0, 0)),
            pl.BlockSpec((1, d), lambda i: (0, 0)),
            pl.BlockSpec((1, 1, d), lambda i: (layer, 0, 2)),
        ],
        out_specs=pl.BlockSpec((OUT_TM, d), lambda i: (i, 0)),
        compiler_params=pltpu.CompilerParams(
            dimension_semantics=("arbitrary",), vmem_limit_bytes=VMEM_LIMIT_BYTES),
        name="outproj",
    )(x2, o_gla, o_rwkv, w_a, w_b, g_row, mod)


FFN_TM = 512
FFN_TF = 1024


def _ffn_kernel(x_ref, gpre_ref, sh_ref, sc_ref, w1_ref, w2_ref, gpost_ref, gt_ref, o_ref, h_scr, acc_scr):
    f = pl.program_id(1)

    @pl.when(f == 0)
    def _():
        x = x_ref[...]
        ms = jnp.mean(x * x, axis=-1, keepdims=True)
        y = x * lax.rsqrt(ms + RMS_EPS) * gpre_ref[...]
        h_scr[...] = (y * (1.0 + sc_ref[0]) + sh_ref[0]).astype(BF16)
        acc_scr[...] = jnp.zeros(acc_scr.shape, F32)

    u = jnp.maximum(_dot(h_scr[...], w1_ref[...]), 0.0)
    acc_scr[...] += _dot((u * u).astype(BF16), w2_ref[...])

    @pl.when(f == pl.num_programs(1) - 1)
    def _():
        y = acc_scr[...]
        ms = jnp.mean(y * y, axis=-1, keepdims=True)
        o_ref[...] = x_ref[...] + gt_ref[0] * (y * lax.rsqrt(ms + RMS_EPS) * gpost_ref[...])


def _ffn_call(x2, gpre_row, gpost_row, mod, layer, w1, w2):
    t, d = x2.shape
    dff = w1.shape[1]
    return pl.pallas_call(
        _ffn_kernel,
        out_shape=jax.ShapeDtypeStruct((t, d), F32),
        grid=(t // FFN_TM, dff // FFN_TF),
        in_specs=[
            pl.BlockSpec((FFN_TM, d), lambda i, f: (i, 0)),
            pl.BlockSpec((1, d), lambda i, f: (0, 0)),
            pl.BlockSpec((1, 1, d), lambda i, f: (layer, 0, 3)),
            pl.BlockSpec((1, 1, d), lambda i, f: (layer, 0, 4)),
            pl.BlockSpec((d, FFN_TF), lambda i, f: (0, f)),
            pl.BlockSpec((FFN_TF, d), lambda i, f: (f, 0)),
            pl.BlockSpec((1, d), lambda i, f: (0, 0)),
            pl.BlockSpec((1, 1, d), lambda i, f: (layer, 0, 5)),
        ],
        out_specs=pl.BlockSpec((FFN_TM, d), lambda i, f: (i, 0)),
        scratch_shapes=[pltpu.VMEM((FFN_TM, d), BF16), pltpu.VMEM((FFN_TM, d), F32)],
        compiler_params=pltpu.CompilerParams(
            dimension_semantics=("arbitrary", "arbitrary"), vmem_limit_bytes=VMEM_LIMIT_BYTES),
        name="ffn",
    )(x2, gpre_row, mod, mod, w1, w2, gpost_row, mod)


def _pad_cols(w, width):
    return jnp.pad(w, ((0, 0), (0, width - w.shape[1])))


def _pad_rows(w, rows):
    return jnp.pad(w, ((0, rows - w.shape[0]), (0, 0)))


def _pack_inproj(w_in, mu_rwkv, vres_w_down, vres_mu):
    d = w_in.shape[0]
    gq, gk, gv, gg, ga, rr, rk, rv, rw, ra, rg = jnp.split(
        w_in, np.cumsum([512, 512, 1024, 1024, 16, 1024, 1024, 1024, 96, 96]).tolist(), axis=1)
    mr, mk, mv, mw, ma, mg = jnp.split(mu_rwkv[None, :], np.cumsum([1024, 1024, 1024, 96, 96]).tolist(), axis=1)
    if vres_w_down is None:
        vres_w_down = jnp.zeros((d, RWKV_V_RANK), w_in.dtype)
        vres_mu = jnp.zeros((RWKV_V_RANK,), w_in.dtype)
    w_pack = jnp.concatenate(
        [gq, gk, gv, gg, _pad_cols(ga, LANES), _pad_cols(vres_w_down, LANES),
         rr, rk, rv, rg, _pad_cols(rw, LANES), _pad_cols(ra, LANES)], axis=1).astype(BF16)
    mu_pack = jnp.concatenate(
        [jnp.zeros((1, C_VRES), F32), _pad_cols(vres_mu[None, :], LANES),
         mr, mk, mv, mg, _pad_cols(mw, LANES), _pad_cols(ma, LANES)], axis=1)
    return w_pack, mu_pack


def kernel(x, c, w_ada, b_ada, g_pre_mix, g_post_mix, g_pre_ffn, g_post_ffn, w_in, gla_w_a_up, gla_b_a, gla_norm_w, rwkv_mu, rwkv_w0, rwkv_w_up, rwkv_a0, rwkv_a_up, rwkv_g_up, rwkv_k_k, rwkv_k_a, rwkv_r_k, rwkv_gn_w, rwkv_gn_b, vres_w_down, vres_mu, vres_up, vres_v0, w_out, w_ff1, w_ff2):
    bsz, t, d = x.shape
    assert bsz == 1 and d == D_MODEL and t % (CHUNK * max(GLA_NCH, RWKV_NCH)) == 0
    assert t % FFN_TM == 0 and t % INP_TM == 0 and t % OUT_TM == 0
    n_layers = w_ada.shape[0]

    mod = _ada_call(c.reshape(d, 1), w_ada, b_ada.reshape(n_layers, 1, 6 * d))

    p_gla = jnp.asarray(_gla_exponent_matrix(), BF16)
    gla_masks = tuple(jnp.asarray(m, BF16) for m in _gla_masks())
    tri = jnp.asarray(np.tril(np.ones((CHUNK, CHUNK), np.float32)), BF16)
    head_of_lane = np.arange(RWKV_GW) // RWKV_HEAD
    ones_bd = jnp.asarray((head_of_lane[:, None] == head_of_lane[None, :]).astype(np.float32), BF16)

    x2 = x.reshape(t, d)
    z_first = None
    for i in range(n_layers):
        j = i - 1
        w_pack, mu_pack = _pack_inproj(
            w_in[i], rwkv_mu[i],
            vres_w_down[j] if i > 0 else None, vres_mu[j] if i > 0 else None)
        z = _inproj_call(x2, g_pre_mix[i][None, :], mod, i, w_pack, mu_pack)
        if i == 0:
            z_first = z

        o_gla = _gla_call(
            z, _pad_rows(gla_w_a_up[i], LANES).astype(BF16), gla_b_a[i][None, :], gla_norm_w[i][None, :], p_gla,
            gla_masks)

        row = lambda a: a.reshape(1, RWKV_WIDTH)
        vec = dict(w0=row(rwkv_w0[i]), a0=row(rwkv_a0[i]), k_k=row(rwkv_k_k[i]), k_a=row(rwkv_k_a[i]),
                   r_k=row(rwkv_r_k[i]), gn_w=row(rwkv_gn_w[i]), gn_b=row(rwkv_gn_b[i]))
        mats = dict(w_up=_pad_rows(rwkv_w_up[i], LANES).astype(BF16),
                    a_up=_pad_rows(rwkv_a_up[i], LANES).astype(BF16),
                    g_up=rwkv_g_up[i].astype(BF16))
        if i > 0:
            vec["v0"] = row(vres_v0[j])
            mats["v_up"] = _pad_rows(vres_up[j], LANES).astype(BF16)
        o_rwkv = _rwkv_call(z, z_first, vec, mats, tri, ones_bd, has_vres=i > 0)

        w_o = w_out[i].astype(BF16)
        x2 = _outproj_call(x2, o_gla, o_rwkv, w_o[:GLA_V], w_o[GLA_V:], g_post_mix[i][None, :], mod, i)
        x2 = _ffn_call(x2, g_pre_ffn[i][None, :], g_post_ffn[i][None, :], mod, i,
                       w_ff1[i].astype(BF16), w_ff2[i].astype(BF16))
    return x2.reshape(bsz, t, d)
```

```python
import functools

import numpy as np
import jax
import jax.numpy as jnp
from jax import lax
from jax.experimental import pallas as pl
from jax.experimental.pallas import tpu as pltpu

F32 = jnp.float32
BF16 = jnp.bfloat16

D_MODEL = 2048
DEPTH = 2
GLA_V = 1024
GLA_DV = 128
GLA_HEADS = 8
GLA_DK = 64
GLA_QK = 512
GLA_GATE_RANK = 16
GLA_TAU = 16.0
RWKV_WIDTH = 1024
RWKV_HEAD = 64
RWKV_HEADS = 16
RWKV_W_RANK = 96
RWKV_A_RANK = 96
RWKV_G_RANK = 256
RWKV_V_RANK = 64
RWKV_GN_EPS = 64e-5
D_FF = 4 * D_MODEL
RMS_EPS = 1e-6

LANES = 128
CHUNK = 64
VMEM_LIMIT_BYTES = 56 * 1024 * 1024

C_GQ = 0
C_GK = 512
C_GV = 1024
C_GG = 2048
C_GA = 3072
C_VRES = 3200
C_RR = 3328
C_RK = 4352
C_RV = 5376
C_RG = 6400
C_RW = 6656
C_RA = 6784
N_PACK = 6912


def _dot(a, b):
    return jnp.dot(a, b, preferred_element_type=F32)


def _dot_nt(a, b):
    return lax.dot_general(a, b, (((1,), (1,)), ((), ())), preferred_element_type=F32)


def _dot_split2_lhs(p, x):
    hi = x.astype(BF16)
    lo = (x - hi.astype(F32)).astype(BF16)
    y = _dot(p, jnp.concatenate([hi, lo], axis=1))
    return y[:, :x.shape[1]] + y[:, x.shape[1]:]


def _dot_split2_rhs(x, p):
    hi = x.astype(BF16)
    lo = (x - hi.astype(F32)).astype(BF16)
    y = _dot(jnp.concatenate([hi, lo], axis=0), p)
    return y[:x.shape[0]] + y[x.shape[0]:]


def _log_sigmoid(x):
    return jnp.minimum(x, 0.0) - jnp.log(1.0 + jnp.exp(-jnp.abs(x)))


def _sigmoid(x):
    return 1.0 / (1.0 + jnp.exp(-x))


ADA_TN = 1024


def _ada_kernel(c_ref, w_ref, b_ref, o_ref):
    c = c_ref[...]
    cond = c * _sigmoid(c)
    o_ref[0] = jnp.sum(w_ref[0] * cond, axis=0, keepdims=True) + b_ref[0]


def _ada_call(c_col, w_ada, b_ada3):
    n_layers, d, n = w_ada.shape
    return pl.pallas_call(
        _ada_kernel,
        out_shape=jax.ShapeDtypeStruct((n_layers, 1, n), F32),
        grid=(n_layers, n // ADA_TN),
        in_specs=[
            pl.BlockSpec((d, 1), lambda l, j: (0, 0)),
            pl.BlockSpec((1, d, ADA_TN), lambda l, j: (l, 0, j)),
            pl.BlockSpec((1, 1, ADA_TN), lambda l, j: (l, 0, j)),
        ],
        out_specs=pl.BlockSpec((1, 1, ADA_TN), lambda l, j: (l, 0, j)),
        compiler_params=pltpu.CompilerParams(
            dimension_semantics=("arbitrary", "arbitrary"), vmem_limit_bytes=VMEM_LIMIT_BYTES),
        name="ada_mod",
    )(c_col, w_ada, b_ada3)


INP_TM = 1024
INP_TN = 768


def _inproj_kernel(x_ref, g_ref, sh_ref, sc_ref, w_ref, mu_ref, z_ref, h_scr, carry_scr):
    i = pl.program_id(0)
    j = pl.program_id(1)

    @pl.when(j == 0)
    def _():
        x = x_ref[...]
        ms = jnp.mean(x * x, axis=-1, keepdims=True)
        y = x * lax.rsqrt(ms + RMS_EPS) * g_ref[...]
        h_scr[...] = (y * (1.0 + sc_ref[0]) + sh_ref[0]).astype(BF16)

    @pl.when(i == 0)
    def _():
        carry_scr[j] = jnp.zeros(carry_scr.shape[1:], F32)

    z = _dot_nt(h_scr[...], w_ref[...])
    prev = carry_scr[j]
    row = lax.broadcasted_iota(jnp.int32, z.shape, 0)
    shifted = jnp.where(row == 0, prev, pltpu.roll(z, 1, 0))
    carry_scr[j] = z[z.shape[0] - 1:, :]
    z_ref[...] = (z + mu_ref[...] * (shifted - z)).astype(z_ref.dtype)


def _inproj_call(x2, g_row, mod, layer, w_pack_t, mu_pack):
    t, d = x2.shape
    n = w_pack_t.shape[0]
    nj = n // INP_TN
    return pl.pallas_call(
        _inproj_kernel,
        out_shape=jax.ShapeDtypeStruct((t, n), BF16),
        grid=(t // INP_TM, nj),
        in_specs=[
            pl.BlockSpec((INP_TM, d), lambda i, j: (i, 0)),
            pl.BlockSpec((1, d), lambda i, j: (0, 0)),
            pl.BlockSpec((1, 1, d), lambda i, j: (layer, 0, 0)),
            pl.BlockSpec((1, 1, d), lambda i, j: (layer, 0, 1)),
            pl.BlockSpec((INP_TN, d), lambda i, j: (j, 0)),
            pl.BlockSpec((1, INP_TN), lambda i, j: (0, j)),
        ],
        out_specs=pl.BlockSpec((INP_TM, INP_TN), lambda i, j: (i, j)),
        scratch_shapes=[pltpu.VMEM((INP_TM, d), BF16), pltpu.VMEM((nj, 1, INP_TN), F32)],
        compiler_params=pltpu.CompilerParams(
            dimension_semantics=("arbitrary", "arbitrary"), vmem_limit_bytes=VMEM_LIMIT_BYTES),
        name="inproj",
    )(x2, g_row, mod, mod, w_pack_t, mu_pack)


GLA_NCH = 8
GLA_GROUP = 4
GLA_QW = GLA_GROUP * GLA_DK
GLA_VW = GLA_GROUP * GLA_DV
GLA_LEVELS = (32, 16, 8, 4, 2, 1)


def _gla_exponent_matrix():
    c = CHUNK
    p = np.zeros((2 * c + len(GLA_LEVELS) * c, c), np.float32)
    for i in range(c):
        p[i, : i + 1] = 1.0
        p[c + i, i + 1:] = 1.0
    for li, s in enumerate(GLA_LEVELS):
        base = 2 * c + li * c
        for i in range(c):
            m = (i // (2 * s)) * (2 * s) + s
            if i & s:
                p[base + i, m + 1: i + 1] = 1.0
            else:
                p[base + i, i + 1: m + 1] = 1.0
    return p


def _gla_masks():
    hq = np.arange(GLA_QW) // GLA_DK
    hv = np.arange(GLA_VW) // GLA_DV
    qq = (hq[:, None] == hq[None, :]).astype(np.float32)
    qv = (hq[:, None] == hv[None, :]).astype(np.float32)
    return qq, qv, qv.T.copy()


def _gla_kernel(q_ref, k_ref, v_ref, g_ref, al_ref, wup_ref, ba_ref, nw_ref, p_ref, mqq_ref, mqv_ref, mvq_ref,
                o_ref, st_scr):
    @pl.when(pl.program_id(1) == 0)
    def _():
        st_scr[...] = jnp.zeros(st_scr.shape, F32)

    c = CHUNK
    qw, vw = GLA_QW, GLA_VW
    tt = lax.broadcasted_iota(jnp.int32, (c, qw), 0)
    jj = lax.broadcasted_iota(jnp.int32, (c, qw), 1) & (GLA_DK - 1)
    mqq, mqv, mvq = mqq_ref[...], mqv_ref[...], mvq_ref[...]
    same_head_vq = ((lax.broadcasted_iota(jnp.int32, (vw, qw), 0) >> int(np.log2(GLA_DV)))
                    == (lax.broadcasted_iota(jnp.int32, (vw, qw), 1) >> int(np.log2(GLA_DK))))

    def bd_qk(x):
        return jnp.concatenate([x.astype(BF16)] * GLA_GROUP, axis=0) * mqq

    x = _dot(al_ref[...].astype(BF16), wup_ref[...]) + ba_ref[...]
    la_all = _log_sigmoid(x) * (1.0 / GLA_TAU)
    q_all = q_ref[...].astype(F32) * (GLA_DK ** -0.5)
    k_all = k_ref[...].astype(F32)
    v_all = v_ref[...].astype(F32)

    chunks = range(GLA_NCH)
    rows = [slice(ci * c, (ci + 1) * c) for ci in chunks]
    e_all = [_dot_split2_lhs(p_ref[...], la_all[rw]) for rw in rows]
    q = [q_all[rw] for rw in rows]
    k = [k_all[rw] for rw in rows]

    scores = [jnp.where(tt == jj, _dot_nt(q[ci].astype(BF16), bd_qk(k[ci])), 0.0) for ci in chunks]
    for li, s in enumerate(GLA_LEVELS):
        second = (tt & s) != 0
        shift = int(np.log2(2 * s))
        same = (tt >> shift) == (jj >> shift)
        e = [jnp.exp(e_all[ci][(2 + li) * c:(3 + li) * c]) for ci in chunks]
        qd = [jnp.where(second, q[ci] * e[ci], 0.0).astype(BF16) for ci in chunks]
        kd = [bd_qk(jnp.where(second, 0.0, k[ci] * e[ci])) for ci in chunks]
        scores = [scores[ci] + jnp.where(same, _dot_nt(qd[ci], kd[ci]), 0.0) for ci in chunks]

    v_bd = [jnp.concatenate([v_all[rw].astype(BF16)] * GLA_GROUP, axis=0) * mqv for rw in rows]
    o_intra = [_dot(scores[ci].astype(BF16), v_bd[ci]) for ci in chunks]
    upd = []
    for ci in chunks:
        ke = (k[ci] * jnp.exp(e_all[ci][c:2 * c])).astype(BF16)
        m = jnp.where(same_head_vq, _dot(v_all[rows[ci]].T.astype(BF16), ke), 0.0)
        dv = GLA_DV
        upd.append(m[0:dv] + m[dv:2 * dv] + m[2 * dv:3 * dv] + m[3 * dv:4 * dv])

    st = st_scr[...]
    os_ = []
    for ci in chunks:
        b = e_all[ci][0:c]
        st_bd = jnp.concatenate([st.astype(BF16)] * GLA_GROUP, axis=0) * mvq
        os_.append(_dot_nt((q[ci] * jnp.exp(b)).astype(BF16), st_bd) + o_intra[ci])
        st = st * jnp.exp(b[c - 1:c, :]) + upd[ci]
    st_scr[...] = st

    o = jnp.concatenate(os_, axis=0)
    g = g_ref[...].astype(F32)
    gs = g * _sigmoid(g) * nw_ref[...]
    outs = []
    for h in range(GLA_GROUP):
        oh = o[:, h * GLA_DV:(h + 1) * GLA_DV]
        oh = oh * lax.rsqrt(jnp.mean(oh * oh, axis=-1, keepdims=True) + RMS_EPS)
        outs.append(oh * gs[:, h * GLA_DV:(h + 1) * GLA_DV])
    o_ref[...] = jnp.concatenate(outs, axis=1).astype(o_ref.dtype)


def _gla_call(z, wup_pad, ba_row, nw_row, p_mat, masks):
    t = z.shape[0]
    tb = CHUNK * GLA_NCH
    groups = GLA_HEADS // GLA_GROUP
    blk = lambda w, off: pl.BlockSpec((tb, w), lambda p, c: (c, off // w + p))
    const = lambda a: pl.BlockSpec(a.shape, lambda p, c: (0, 0))
    return pl.pallas_call(
        _gla_kernel,
        out_shape=jax.ShapeDtypeStruct((t, GLA_V), BF16),
        grid=(groups, t // tb),
        in_specs=[
            blk(GLA_QW, C_GQ), blk(GLA_QW, C_GK), blk(GLA_VW, C_GV), blk(GLA_VW, C_GG),
            pl.BlockSpec((tb, LANES), lambda p, c: (c, C_GA // LANES)),
            pl.BlockSpec((LANES, GLA_QW), lambda p, c: (0, p)),
            pl.BlockSpec((1, GLA_QW), lambda p, c: (0, p)),
            pl.BlockSpec((1, GLA_VW), lambda p, c: (0, p)),
            const(p_mat), const(masks[0]), const(masks[1]), const(masks[2]),
        ],
        out_specs=pl.BlockSpec((tb, GLA_VW), lambda p, c: (c, p)),
        scratch_shapes=[pltpu.VMEM((GLA_DV, GLA_QW), F32)],
        compiler_params=pltpu.CompilerParams(
            dimension_semantics=("arbitrary", "arbitrary"), vmem_limit_bytes=VMEM_LIMIT_BYTES),
        name="gla_mixer",
    )(z, z, z, z, z, wup_pad, ba_row, nw_row, p_mat, *masks)


RWKV_NCH = 8
RWKV_GROUP = 4
RWKV_GW = RWKV_GROUP * RWKV_HEAD
RWKV_INV_LEVELS = (2, 4, 8, 16, 32)


def _rwkv_kernel(*refs, has_vres):
    if has_vres:
        (r_ref, k_ref, v_ref, wl_ref, al_ref, gl_ref, vl_ref, vf_ref, vup_ref, v0_ref,
         w0_ref, wup_ref, a0_ref, aup_ref, gup_ref, kk_ref, ka_ref, rk_ref, gnw_ref, gnb_ref,
         tri_ref, ones_ref, o_ref, st_scr) = refs
    else:
        (r_ref, k_ref, v_ref, wl_ref, al_ref, gl_ref,
         w0_ref, wup_ref, a0_ref, aup_ref, gup_ref, kk_ref, ka_ref, rk_ref, gnw_ref, gnb_ref,
         tri_ref, ones_ref, o_ref, st_scr) = refs

    @pl.when(pl.program_id(1) == 0)
    def _():
        st_scr[...] = jnp.zeros(st_scr.shape, F32)

    c = CHUNK
    gw = RWKV_GW
    tt = lax.broadcasted_iota(jnp.int32, (c, gw), 0)
    jj = lax.broadcasted_iota(jnp.int32, (c, gw), 1) & (RWKV_HEAD - 1)
    strict = tt > jj
    incl = tt >= jj
    ones_bd = ones_ref[...]
    same_head_b = ones_bd != 0
    hshift = int(np.log2(RWKV_HEAD))
    same_head = ((lax.broadcasted_iota(jnp.int32, (gw, gw), 0) >> hshift)
                 == (lax.broadcasted_iota(jnp.int32, (gw, gw), 1) >> hshift))
    inv_n = 1.0 / RWKV_HEAD

    def seg_sum(x):
        return _dot_split2_rhs(x, ones_bd)

    def block_diag(x):
        return jnp.concatenate([x.astype(BF16)] * RWKV_GROUP, axis=0) * ones_bd

    tb = c * RWKV_NCH
    r_all = r_ref[...].astype(F32)
    k_all = k_ref[...].astype(F32)
    v_all = v_ref[...].astype(F32)
    w_pre = w0_ref[...] + _dot(jnp.tanh(wl_ref[...].astype(F32)).astype(BF16), wup_ref[...])
    lw_all = -jnp.exp(_log_sigmoid(w_pre) - 0.5)
    alr = _sigmoid(a0_ref[...] + _dot(al_ref[...].astype(BF16), aup_ref[...]))
    gate_all = _dot(_sigmoid(gl_ref[...].astype(F32)).astype(BF16), gup_ref[...])
    if has_vres:
        mix = _sigmoid(v0_ref[...] + _dot(vl_ref[...].astype(BF16), vup_ref[...]))
        v_all = v_all + (vf_ref[...].astype(F32) - v_all) * mix
    kk = k_all * kk_ref[...]
    k2_all = k_all * (1.0 + (alr - 1.0) * ka_ref[...])
    sums = seg_sum(jnp.concatenate([kk * kk, r_all * k2_all * rk_ref[...]], axis=0))
    kk = kk * lax.rsqrt(jnp.maximum(sums[:tb], 1e-24))
    bonus_all = sums[tb:] * v_all
    a_all = -kk
    b_all = kk * alr

    chunks = range(RWKV_NCH)
    rows = [slice(ci * c, (ci + 1) * c) for ci in chunks]
    cl = [_dot_split2_lhs(tri_ref[...], lw_all[rw]) for rw in rows]
    pre = []
    for ci in chunks:
        rw = rows[ci]
        cl_last = cl[ci][c - 1:c, :]
        e_neg = jnp.exp(-cl[ci])
        e_end = jnp.exp(cl_last - cl[ci])
        b, k2, v = b_all[rw], k2_all[rw], v_all[rw]
        at = a_all[rw] * jnp.exp(cl[ci] - lw_all[rw])
        rt = r_all[rw] * jnp.exp(cl[ci])
        pre.append(dict(
            rt=rt, v=v, at=at, g_c=jnp.exp(cl_last),
            lhs=jnp.concatenate([at, rt], axis=0).astype(BF16),
            rhs=jnp.concatenate([block_diag(b * e_neg), block_diag(k2 * e_neg)], axis=0),
            hat=jnp.concatenate([b * e_end, k2 * e_end], axis=0).astype(BF16)))
    for p in pre:
        aa = _dot_nt(p["lhs"], p["rhs"])
        p.update(
            a_ab=jnp.where(strict, aa[:c, :gw], 0.0),
            a_ak=jnp.where(strict, aa[:c, gw:], 0.0).astype(BF16),
            p_rb=jnp.where(incl, aa[c:, :gw], 0.0).astype(BF16),
            p_rk=jnp.where(incl, aa[c:, gw:], 0.0).astype(BF16))

    a_bd = [block_diag(p["a_ab"]) for p in pre]
    tinv = [jnp.where(tt == jj, 1.0, jnp.where((tt >> 1) == (jj >> 1), p["a_ab"], 0.0)) for p in pre]
    for s in RWKV_INV_LEVELS:
        shift = int(np.log2(2 * s))
        off = ((tt >> shift) == (jj >> shift)) & ((tt & s) != 0) & ((jj & s) == 0)
        t_bd = [block_diag(t) for t in tinv]
        half = [jnp.where(off, _dot(tinv[ci].astype(BF16), a_bd[ci]), 0.0).astype(BF16) for ci in chunks]
        tinv = [tinv[ci] + _dot(half[ci], t_bd[ci]) for ci in chunks]

    t_b = [t.astype(BF16) for t in tinv]
    v_bd = [block_diag(p["v"]) for p in pre]
    wt = [_dot(t_b[ci], block_diag(pre[ci]["at"])) for ci in chunks]
    akv = [block_diag(_dot(pre[ci]["a_ak"], v_bd[ci])) for ci in chunks]
    u0 = [_dot(t_b[ci], akv[ci]) for ci in chunks]
    qh = [(pre[ci]["rt"] + _dot(pre[ci]["p_rb"], block_diag(wt[ci]))).astype(BF16) for ci in chunks]
    gmat = [jnp.where(same_head, _dot(wt[ci].T.astype(BF16), pre[ci]["hat"][:c]), 0.0).astype(BF16)
            for ci in chunks]
    y0 = [_dot(pre[ci]["p_rb"], block_diag(u0[ci])) + _dot(pre[ci]["p_rk"], v_bd[ci]) for ci in chunks]
    n0c = []
    for ci in chunks:
        uv_t = jnp.concatenate([u0[ci], pre[ci]["v"]], axis=0).T.astype(BF16)
        n0 = jnp.where(same_head, _dot(uv_t, pre[ci]["hat"]), 0.0)
        n0c.append(n0[0:c] + n0[c:2 * c] + n0[2 * c:3 * c] + n0[3 * c:4 * c])

    st = st_scr[...]
    ys = []
    for ci in chunks:
        ys.append(_dot_nt(qh[ci], block_diag(st)) + y0[ci])
        st = st * pre[ci]["g_c"] + _dot(st.astype(BF16), gmat[ci]) + n0c[ci]
    st_scr[...] = st

    y = jnp.concatenate(ys, axis=0)
    yc = y - seg_sum(y) * inv_n
    var = seg_sum(yc * yc) * inv_n
    yn = yc * lax.rsqrt(var + RWKV_GN_EPS) * gnw_ref[...] + gnb_ref[...]
    o_ref[...] = ((yn + bonus_all) * gate_all).astype(o_ref.dtype)


def _rwkv_call(z, z_first, vec, mats, tri, ones_bd, has_vres):
    t = z.shape[0]
    tb = CHUNK * RWKV_NCH
    gw = RWKV_GW
    groups = RWKV_HEADS // RWKV_GROUP
    col = lambda off: pl.BlockSpec((tb, gw), lambda p, c: (c, off // gw + p))
    fixed = lambda off, w: pl.BlockSpec((tb, w), lambda p, c: (c, off // w))
    rowv = pl.BlockSpec((1, gw), lambda p, c: (0, p))
    upm = lambda rows: pl.BlockSpec((rows, gw), lambda p, c: (0, p))
    const = lambda a: pl.BlockSpec(a.shape, lambda p, c: (0, 0))

    in_specs = [col(C_RR), col(C_RK), col(C_RV), fixed(C_RW, LANES), fixed(C_RA, LANES), fixed(C_RG, 2 * LANES)]
    args = [z, z, z, z, z, z]
    if has_vres:
        in_specs += [fixed(C_VRES, LANES), col(C_RV), upm(LANES), rowv]
        args += [z, z_first, mats["v_up"], vec["v0"]]
    in_specs += [rowv, upm(LANES), rowv, upm(LANES), upm(2 * LANES), rowv, rowv, rowv, rowv, rowv,
                 const(tri), const(ones_bd)]
    args += [vec["w0"], mats["w_up"], vec["a0"], mats["a_up"], mats["g_up"], vec["k_k"], vec["k_a"],
             vec["r_k"], vec["gn_w"], vec["gn_b"], tri, ones_bd]
    return pl.pallas_call(
        functools.partial(_rwkv_kernel, has_vres=has_vres),
        out_shape=jax.ShapeDtypeStruct((t, RWKV_WIDTH), BF16),
        grid=(groups, t // tb),
        in_specs=in_specs,
        out_specs=pl.BlockSpec((tb, gw), lambda p, c: (c, p)),
        scratch_shapes=[pltpu.VMEM((CHUNK, RWKV_GW), F32)],
        compiler_params=pltpu.CompilerParams(
            dimension_semantics=("arbitrary", "arbitrary"), vmem_limit_bytes=VMEM_LIMIT_BYTES),
        name="rwkv7_mixer",
    )(*args)


OUT_TM = 512
OUT_SUB = 256


def _outproj_kernel(x_ref, oa_ref, ob_ref, wa_ref, wb_ref, g_ref, gt_ref, gpre_ref, sh_ref, sc_ref, o_ref, h_ref):
    subs = [slice(s, s + OUT_SUB) for s in range(0, OUT_TM, OUT_SUB)]
    ys = [_dot(oa_ref[rw, :], wa_ref[...]) + _dot(ob_ref[rw, :], wb_ref[...]) for rw in subs]
    post_scale = gt_ref[0] * g_ref[...]
    pre_scale = gpre_ref[...] * (1.0 + sc_ref[0])
    for rw, y in zip(subs, ys):
        ms = jnp.mean(y * y, axis=-1, keepdims=True)
        xn = x_ref[rw, :] + (y * lax.rsqrt(ms + RMS_EPS)) * post_scale
        o_ref[rw, :] = xn
        ms2 = jnp.mean(xn * xn, axis=-1, keepdims=True)
        h_ref[rw, :] = ((xn * lax.rsqrt(ms2 + RMS_EPS)) * pre_scale + sh_ref[0]).astype(h_ref.dtype)


def _outproj_call(x2, o_gla, o_rwkv, w_a, w_b, g_row, gpre_row, mod, layer):
    t, d = x2.shape
    return pl.pallas_call(
        _outproj_kernel,
        out_shape=(jax.ShapeDtypeStruct((t, d), F32), jax.ShapeDtypeStruct((t, d), BF16)),
        grid=(t // OUT_TM,),
        in_specs=[
            pl.BlockSpec((OUT_TM, d), lambda i: (i, 0)),
            pl.BlockSpec((OUT_TM, GLA_V), lambda i: (i, 0)),
            pl.BlockSpec((OUT_TM, RWKV_WIDTH), lambda i: (i, 0)),
            pl.BlockSpec((GLA_V, d), lambda i: (0, 0)),
            pl.BlockSpec((RWKV_WIDTH, d), lambda i: (0, 0)),
            pl.BlockSpec((1, d), lambda i: (0, 0)),
            pl.BlockSpec((1, 1, d), lambda i: (layer, 0, 2)),
            pl.BlockSpec((1, d), lambda i: (0, 0)),
            pl.BlockSpec((1, 1, d), lambda i: (layer, 0, 3)),
            pl.BlockSpec((1, 1, d), lambda i: (layer, 0, 4)),
        ],
        out_specs=(pl.BlockSpec((OUT_TM, d), lambda i: (i, 0)), pl.BlockSpec((OUT_TM, d), lambda i: (i, 0))),
        compiler_params=pltpu.CompilerParams(
            dimension_semantics=("arbitrary",), vmem_limit_bytes=VMEM_LIMIT_BYTES),
        name="outproj",
    )(x2, o_gla, o_rwkv, w_a, w_b, g_row, mod, gpre_row, mod, mod)


FFN_TM = 512
FFN_TF = 1024


def _ffn_kernel(x_ref, h_ref, w1_ref, w2_ref, gpost_ref, gt_ref, o_ref, acc_scr):
    f = pl.program_id(1)

    @pl.when(f == 0)
    def _():
        acc_scr[...] = jnp.zeros(acc_scr.shape, F32)

    u = jnp.maximum(_dot(h_ref[...], w1_ref[...]), 0.0)
    acc_scr[...] += _dot((u * u).astype(BF16), w2_ref[...])

    @pl.when(f == pl.num_programs(1) - 1)
    def _():
        y = acc_scr[...]
        ms = jnp.mean(y * y, axis=-1, keepdims=True)
        o_ref[...] = x_ref[...] + gt_ref[0] * (y * lax.rsqrt(ms + RMS_EPS) * gpost_ref[...])


def _ffn_call(x2, h2, gpost_row, mod, layer, w1, w2):
    t, d = x2.shape
    dff = w1.shape[1]
    return pl.pallas_call(
        _ffn_kernel,
        out_shape=jax.ShapeDtypeStruct((t, d), F32),
        grid=(t // FFN_TM, dff // FFN_TF),
        in_specs=[
            pl.BlockSpec((FFN_TM, d), lambda i, f: (i, 0)),
            pl.BlockSpec((FFN_TM, d), lambda i, f: (i, 0)),
            pl.BlockSpec((d, FFN_TF), lambda i, f: (0, f)),
            pl.BlockSpec((FFN_TF, d), lambda i, f: (f, 0)),
            pl.BlockSpec((1, d), lambda i, f: (0, 0)),
            pl.BlockSpec((1, 1, d), lambda i, f: (layer, 0, 5)),
        ],
        out_specs=pl.BlockSpec((FFN_TM, d), lambda i, f: (i, 0)),
        scratch_shapes=[pltpu.VMEM((FFN_TM, d), F32)],
        compiler_params=pltpu.CompilerParams(
            dimension_semantics=("arbitrary", "arbitrary"), vmem_limit_bytes=VMEM_LIMIT_BYTES),
        name="ffn",
    )(x2, h2, w1, w2, gpost_row, mod)


def _pad_cols(w, width):
    return jnp.pad(w, ((0, 0), (0, width - w.shape[1])))


def _pad_rows(w, rows):
    return jnp.pad(w, ((0, rows - w.shape[0]), (0, 0)))


def _pack_inproj(w_in_t, mu_rwkv, vres_w_down_t, vres_mu):
    d = w_in_t.shape[1]
    gq, gk, gv, gg, ga, rr, rk, rv, rw, ra, rg = jnp.split(
        w_in_t, np.cumsum([512, 512, 1024, 1024, 16, 1024, 1024, 1024, 96, 96]).tolist(), axis=0)
    mr, mk, mv, mw, ma, mg = jnp.split(mu_rwkv[None, :], np.cumsum([1024, 1024, 1024, 96, 96]).tolist(), axis=1)
    if vres_w_down_t is None:
        vres_w_down_t = jnp.zeros((RWKV_V_RANK, d), w_in_t.dtype)
        vres_mu = jnp.zeros((RWKV_V_RANK,), F32)
    w_pack_t = jnp.concatenate(
        [gq, gk, gv, gg, _pad_rows(ga, LANES), _pad_rows(vres_w_down_t, LANES),
         rr, rk, rv, rg, _pad_rows(rw, LANES), _pad_rows(ra, LANES)], axis=0)
    mu_pack = jnp.concatenate(
        [jnp.zeros((1, C_VRES), F32), _pad_cols(vres_mu[None, :], LANES),
         mr, mk, mv, mg, _pad_cols(mw, LANES), _pad_cols(ma, LANES)], axis=1)
    return w_pack_t, mu_pack


def kernel(x, c, w_ada, b_ada, g_pre_mix, g_post_mix, g_pre_ffn, g_post_ffn, w_in, gla_w_a_up, gla_b_a, gla_norm_w, rwkv_mu, rwkv_w0, rwkv_w_up, rwkv_a0, rwkv_a_up, rwkv_g_up, rwkv_k_k, rwkv_k_a, rwkv_r_k, rwkv_gn_w, rwkv_gn_b, vres_w_down, vres_mu, vres_up, vres_v0, w_out, w_ff1, w_ff2):
    bsz, t, d = x.shape
    assert bsz == 1 and d == D_MODEL and t % (CHUNK * max(GLA_NCH, RWKV_NCH)) == 0
    assert t % FFN_TM == 0 and t % INP_TM == 0 and t % OUT_TM == 0
    n_layers = w_ada.shape[0]

    mod = _ada_call(c.reshape(d, 1), w_ada, b_ada.reshape(n_layers, 1, 6 * d))

    p_gla = jnp.asarray(_gla_exponent_matrix(), BF16)
    gla_masks = tuple(jnp.asarray(m, BF16) for m in _gla_masks())
    tri = jnp.asarray(np.tril(np.ones((CHUNK, CHUNK), np.float32)), BF16)
    head_of_lane = np.arange(RWKV_GW) // RWKV_HEAD
    ones_bd = jnp.asarray((head_of_lane[:, None] == head_of_lane[None, :]).astype(np.float32), BF16)

    w_in_t = jnp.swapaxes(w_in, 1, 2).astype(BF16)
    x2 = x.reshape(t, d)
    z_first = None
    for i in range(n_layers):
        j = i - 1
        w_pack_t, mu_pack = _pack_inproj(
            w_in_t[i], rwkv_mu[i],
            vres_w_down[j].T.astype(BF16) if i > 0 else None, vres_mu[j] if i > 0 else None)
        z = _inproj_call(x2, g_pre_mix[i][None, :], mod, i, w_pack_t, mu_pack)
        if i == 0:
            z_first = z

        o_gla = _gla_call(
            z, _pad_rows(gla_w_a_up[i], LANES).astype(BF16), gla_b_a[i][None, :], gla_norm_w[i][None, :], p_gla,
            gla_masks)

        row = lambda a: a.reshape(1, RWKV_WIDTH)
        vec = dict(w0=row(rwkv_w0[i]), a0=row(rwkv_a0[i]), k_k=row(rwkv_k_k[i]), k_a=row(rwkv_k_a[i]),
                   r_k=row(rwkv_r_k[i]), gn_w=row(rwkv_gn_w[i]), gn_b=row(rwkv_gn_b[i]))
        mats = dict(w_up=_pad_rows(rwkv_w_up[i], LANES).astype(BF16),
                    a_up=_pad_rows(rwkv_a_up[i], LANES).astype(BF16),
                    g_up=rwkv_g_up[i].astype(BF16))
        if i > 0:
            vec["v0"] = row(vres_v0[j])
            mats["v_up"] = _pad_rows(vres_up[j], LANES).astype(BF16)
        o_rwkv = _rwkv_call(z, z_first, vec, mats, tri, ones_bd, has_vres=i > 0)

        w_o = w_out[i].astype(BF16)
        x2, h2 = _outproj_call(x2, o_gla, o_rwkv, w_o[:GLA_V], w_o[GLA_V:], g_post_mix[i][None, :],
                               g_pre_ffn[i][None, :], mod, i)
        x2 = _ffn_call(x2, h2, g_post_ffn[i][None, :], mod, i, w_ff1[i].astype(BF16), w_ff2[i].astype(BF16))
    return x2.reshape(bsz, t, d)
```

```python
import functools

import numpy as np
import jax
import jax.numpy as jnp
from jax import lax
from jax.experimental import pallas as pl
from jax.experimental.pallas import tpu as pltpu

F32 = jnp.float32
BF16 = jnp.bfloat16

D_MODEL = 2048
DEPTH = 2
GLA_V = 1024
GLA_DV = 128
GLA_HEADS = 8
GLA_DK = 64
GLA_QK = 512
GLA_GATE_RANK = 16
GLA_TAU = 16.0
RWKV_WIDTH = 1024
RWKV_HEAD = 64
RWKV_HEADS = 16
RWKV_W_RANK = 96
RWKV_A_RANK = 96
RWKV_G_RANK = 256
RWKV_V_RANK = 64
RWKV_GN_EPS = 64e-5
D_FF = 4 * D_MODEL
RMS_EPS = 1e-6

LANES = 128
CHUNK = 64
VMEM_LIMIT_BYTES = 56 * 1024 * 1024

C_GQ = 0
C_GK = 512
C_GV = 1024
C_GG = 2048
C_GA = 3072
C_VRES = 3200
C_RR = 3328
C_RK = 4352
C_RV = 5376
C_RL = 6400
N_PACK = 6912
LORA_W = (0, RWKV_W_RANK)
LORA_A = (RWKV_W_RANK, RWKV_W_RANK + RWKV_A_RANK)
LORA_G = (RWKV_W_RANK + RWKV_A_RANK, RWKV_W_RANK + RWKV_A_RANK + RWKV_G_RANK)


def _dot(a, b):
    return jnp.dot(a, b, preferred_element_type=F32)


def _dot_nt(a, b):
    return lax.dot_general(a, b, (((1,), (1,)), ((), ())), preferred_element_type=F32)


def _dot_split2_lhs(p, x):
    hi = x.astype(BF16)
    lo = (x - hi.astype(F32)).astype(BF16)
    y = _dot(p, jnp.concatenate([hi, lo], axis=1))
    return y[:, :x.shape[1]] + y[:, x.shape[1]:]


def _dot_split2_rhs(x, p):
    hi = x.astype(BF16)
    lo = (x - hi.astype(F32)).astype(BF16)
    y = _dot(jnp.concatenate([hi, lo], axis=0), p)
    return y[:x.shape[0]] + y[x.shape[0]:]


def _log_sigmoid(x):
    return jnp.minimum(x, 0.0) - jnp.log(1.0 + jnp.exp(-jnp.abs(x)))


def _sigmoid(x):
    return 1.0 / (1.0 + jnp.exp(-x))


ADA_TN = 1024


def _ada_kernel(c_ref, w_ref, b_ref, o_ref):
    c = c_ref[...]
    cond = c * _sigmoid(c)
    o_ref[0] = jnp.sum(w_ref[0] * cond, axis=0, keepdims=True) + b_ref[0]


def _ada_call(c_col, w_ada, b_ada3):
    n_layers, d, n = w_ada.shape
    return pl.pallas_call(
        _ada_kernel,
        out_shape=jax.ShapeDtypeStruct((n_layers, 1, n), F32),
        grid=(n_layers, n // ADA_TN),
        in_specs=[
            pl.BlockSpec((d, 1), lambda l, j: (0, 0)),
            pl.BlockSpec((1, d, ADA_TN), lambda l, j: (l, 0, j)),
            pl.BlockSpec((1, 1, ADA_TN), lambda l, j: (l, 0, j)),
        ],
        out_specs=pl.BlockSpec((1, 1, ADA_TN), lambda l, j: (l, 0, j)),
        compiler_params=pltpu.CompilerParams(
            dimension_semantics=("arbitrary", "arbitrary"), vmem_limit_bytes=VMEM_LIMIT_BYTES),
        name="ada_mod",
    )(c_col, w_ada, b_ada3)


INP_TM = 1024
INP_TN = 768


def _inproj_kernel(x_ref, g_ref, sh_ref, sc_ref, w_ref, mu_ref, z_ref, h_scr, carry_scr):
    i = pl.program_id(0)
    j = pl.program_id(1)

    @pl.when(j == 0)
    def _():
        x = x_ref[...]
        ms = jnp.mean(x * x, axis=-1, keepdims=True)
        y = x * lax.rsqrt(ms + RMS_EPS) * g_ref[...]
        h_scr[...] = (y * (1.0 + sc_ref[0]) + sh_ref[0]).astype(BF16)

    @pl.when(i == 0)
    def _():
        carry_scr[j] = jnp.zeros(carry_scr.shape[1:], F32)

    z = _dot_nt(h_scr[...], w_ref[...])
    prev = carry_scr[j]
    row = lax.broadcasted_iota(jnp.int32, z.shape, 0)
    shifted = jnp.where(row == 0, prev, pltpu.roll(z, 1, 0))
    carry_scr[j] = z[z.shape[0] - 1:, :]
    z_ref[...] = (z + mu_ref[...] * (shifted - z)).astype(z_ref.dtype)


def _inproj_call(x2, g_row, mod, layer, w_pack_t, mu_pack):
    t, d = x2.shape
    n = w_pack_t.shape[0]
    nj = n // INP_TN
    return pl.pallas_call(
        _inproj_kernel,
        out_shape=jax.ShapeDtypeStruct((t, n), BF16),
        grid=(t // INP_TM, nj),
        in_specs=[
            pl.BlockSpec((INP_TM, d), lambda i, j: (i, 0)),
            pl.BlockSpec((1, d), lambda i, j: (0, 0)),
            pl.BlockSpec((1, 1, d), lambda i, j: (layer, 0, 0)),
            pl.BlockSpec((1, 1, d), lambda i, j: (layer, 0, 1)),
            pl.BlockSpec((INP_TN, d), lambda i, j: (j, 0)),
            pl.BlockSpec((1, INP_TN), lambda i, j: (0, j)),
        ],
        out_specs=pl.BlockSpec((INP_TM, INP_TN), lambda i, j: (i, j)),
        scratch_shapes=[pltpu.VMEM((INP_TM, d), BF16), pltpu.VMEM((nj, 1, INP_TN), F32)],
        compiler_params=pltpu.CompilerParams(
            dimension_semantics=("arbitrary", "arbitrary"), vmem_limit_bytes=VMEM_LIMIT_BYTES),
        name="inproj",
    )(x2, g_row, mod, mod, w_pack_t, mu_pack)


GLA_NCH = 8
GLA_GROUP = 4
GLA_QW = GLA_GROUP * GLA_DK
GLA_VW = GLA_GROUP * GLA_DV
GLA_LEVELS = (32, 16, 8, 4, 2, 1)


def _gla_exponent_matrix():
    c = CHUNK
    p = np.zeros((2 * c + len(GLA_LEVELS) * c, c), np.float32)
    for i in range(c):
        p[i, : i + 1] = 1.0
        p[c + i, i + 1:] = 1.0
    for li, s in enumerate(GLA_LEVELS):
        base = 2 * c + li * c
        for i in range(c):
            m = (i // (2 * s)) * (2 * s) + s
            if i & s:
                p[base + i, m + 1: i + 1] = 1.0
            else:
                p[base + i, i + 1: m + 1] = 1.0
    return p


def _gla_masks():
    hq = np.arange(GLA_QW) // GLA_DK
    hv = np.arange(GLA_VW) // GLA_DV
    qq = (hq[:, None] == hq[None, :]).astype(np.float32)
    qv = (hq[:, None] == hv[None, :]).astype(np.float32)
    return qq, qv, qv.T.copy()


def _gla_kernel(q_ref, k_ref, v_ref, g_ref, al_ref, wup_ref, ba_ref, nw_ref, p_ref, mqq_ref, mqv_ref, mvq_ref,
                o_ref, st_scr):
    @pl.when(pl.program_id(1) == 0)
    def _():
        st_scr[...] = jnp.zeros(st_scr.shape, F32)

    c = CHUNK
    qw, vw = GLA_QW, GLA_VW
    tt = lax.broadcasted_iota(jnp.int32, (c, qw), 0)
    jj = lax.broadcasted_iota(jnp.int32, (c, qw), 1) & (GLA_DK - 1)
    mqq, mqv, mvq = mqq_ref[...], mqv_ref[...], mvq_ref[...]
    same_head_vq = ((lax.broadcasted_iota(jnp.int32, (vw, qw), 0) >> int(np.log2(GLA_DV)))
                    == (lax.broadcasted_iota(jnp.int32, (vw, qw), 1) >> int(np.log2(GLA_DK))))

    def bd_qk(x):
        return jnp.concatenate([x.astype(BF16)] * GLA_GROUP, axis=0) * mqq

    x = _dot(al_ref[...].astype(BF16), wup_ref[...]) + ba_ref[...]
    la_all = _log_sigmoid(x) * (1.0 / GLA_TAU)
    q_all = q_ref[...].astype(F32) * (GLA_DK ** -0.5)
    k_all = k_ref[...].astype(F32)
    v_all = v_ref[...].astype(F32)

    chunks = range(GLA_NCH)
    rows = [slice(ci * c, (ci + 1) * c) for ci in chunks]
    e_all = [_dot_split2_lhs(p_ref[...], la_all[rw]) for rw in rows]
    q = [q_all[rw] for rw in rows]
    k = [k_all[rw] for rw in rows]

    qk_sum = _dot_split2_rhs(q_all * k_all, mqq)
    scores = [jnp.where(tt == jj, qk_sum[rw], 0.0) for rw in rows]
    for li, s in enumerate(GLA_LEVELS):
        second = (tt & s) != 0
        shift = int(np.log2(2 * s))
        same = (tt >> shift) == (jj >> shift)
        e = [jnp.exp(e_all[ci][(2 + li) * c:(3 + li) * c]) for ci in chunks]
        qd = [jnp.where(second, q[ci] * e[ci], 0.0).astype(BF16) for ci in chunks]
        kd = [bd_qk(jnp.where(second, 0.0, k[ci] * e[ci])) for ci in chunks]
        scores = [scores[ci] + jnp.where(same, _dot_nt(qd[ci], kd[ci]), 0.0) for ci in chunks]

    v_bd = [jnp.concatenate([v_all[rw].astype(BF16)] * GLA_GROUP, axis=0) * mqv for rw in rows]
    o_intra = [_dot(scores[ci].astype(BF16), v_bd[ci]) for ci in chunks]
    upd = []
    for ci in chunks:
        ke = (k[ci] * jnp.exp(e_all[ci][c:2 * c])).astype(BF16)
        m = jnp.where(same_head_vq, _dot(v_all[rows[ci]].T.astype(BF16), ke), 0.0)
        dv = GLA_DV
        upd.append(m[0:dv] + m[dv:2 * dv] + m[2 * dv:3 * dv] + m[3 * dv:4 * dv])

    st = st_scr[...]
    os_ = []
    for ci in chunks:
        b = e_all[ci][0:c]
        st_bd = jnp.concatenate([st.astype(BF16)] * GLA_GROUP, axis=0) * mvq
        os_.append(_dot_nt((q[ci] * jnp.exp(b)).astype(BF16), st_bd) + o_intra[ci])
        st = st * jnp.exp(b[c - 1:c, :]) + upd[ci]
    st_scr[...] = st

    o = jnp.concatenate(os_, axis=0)
    g = g_ref[...].astype(F32)
    gs = g * _sigmoid(g) * nw_ref[...]
    outs = []
    for h in range(GLA_GROUP):
        oh = o[:, h * GLA_DV:(h + 1) * GLA_DV]
        oh = oh * lax.rsqrt(jnp.mean(oh * oh, axis=-1, keepdims=True) + RMS_EPS)
        outs.append(oh * gs[:, h * GLA_DV:(h + 1) * GLA_DV])
    o_ref[...] = jnp.concatenate(outs, axis=1).astype(o_ref.dtype)


def _gla_call(z, wup_pad, ba_row, nw_row, p_mat, masks):
    t = z.shape[0]
    tb = CHUNK * GLA_NCH
    groups = GLA_HEADS // GLA_GROUP
    blk = lambda w, off: pl.BlockSpec((tb, w), lambda p, c: (c, off // w + p))
    const = lambda a: pl.BlockSpec(a.shape, lambda p, c: (0, 0))
    return pl.pallas_call(
        _gla_kernel,
        out_shape=jax.ShapeDtypeStruct((t, GLA_V), BF16),
        grid=(groups, t // tb),
        in_specs=[
            blk(GLA_QW, C_GQ), blk(GLA_QW, C_GK), blk(GLA_VW, C_GV), blk(GLA_VW, C_GG),
            pl.BlockSpec((tb, LANES), lambda p, c: (c, C_GA // LANES)),
            pl.BlockSpec((LANES, GLA_QW), lambda p, c: (0, p)),
            pl.BlockSpec((1, GLA_QW), lambda p, c: (0, p)),
            pl.BlockSpec((1, GLA_VW), lambda p, c: (0, p)),
            const(p_mat), const(masks[0]), const(masks[1]), const(masks[2]),
        ],
        out_specs=pl.BlockSpec((tb, GLA_VW), lambda p, c: (c, p)),
        scratch_shapes=[pltpu.VMEM((GLA_DV, GLA_QW), F32)],
        compiler_params=pltpu.CompilerParams(
            dimension_semantics=("arbitrary", "arbitrary"), vmem_limit_bytes=VMEM_LIMIT_BYTES),
        name="gla_mixer",
    )(z, z, z, z, z, wup_pad, ba_row, nw_row, p_mat, *masks)


RWKV_NCH = 8
RWKV_GROUP = 4
RWKV_GW = RWKV_GROUP * RWKV_HEAD
RWKV_INV_LEVELS = (2, 4, 8, 16, 32)


def _rwkv_kernel(*refs, has_vres):
    if has_vres:
        (r_ref, k_ref, v_ref, la_ref, lb_ref, vl_ref, vf_ref, vup_ref, v0_ref,
         w0_ref, wup_ref, a0_ref, aup_ref, gupa_ref, gupb_ref, kk_ref, ka_ref, rk_ref, gnw_ref, gnb_ref,
         tri_ref, ones_ref, o_ref, st_scr) = refs
    else:
        (r_ref, k_ref, v_ref, la_ref, lb_ref,
         w0_ref, wup_ref, a0_ref, aup_ref, gupa_ref, gupb_ref, kk_ref, ka_ref, rk_ref, gnw_ref, gnb_ref,
         tri_ref, ones_ref, o_ref, st_scr) = refs

    @pl.when(pl.program_id(1) == 0)
    def _():
        st_scr[...] = jnp.zeros(st_scr.shape, F32)

    c = CHUNK
    gw = RWKV_GW
    tt = lax.broadcasted_iota(jnp.int32, (c, gw), 0)
    jj = lax.broadcasted_iota(jnp.int32, (c, gw), 1) & (RWKV_HEAD - 1)
    strict = tt > jj
    incl = tt >= jj
    ones_bd = ones_ref[...]
    same_head_b = ones_bd != 0
    hshift = int(np.log2(RWKV_HEAD))
    same_head = ((lax.broadcasted_iota(jnp.int32, (gw, gw), 0) >> hshift)
                 == (lax.broadcasted_iota(jnp.int32, (gw, gw), 1) >> hshift))
    inv_n = 1.0 / RWKV_HEAD

    def seg_sum(x):
        return _dot_split2_rhs(x, ones_bd)

    def block_diag(x):
        return jnp.concatenate([x.astype(BF16)] * RWKV_GROUP, axis=0) * ones_bd

    tb = c * RWKV_NCH
    r_all = r_ref[...].astype(F32)
    k_all = k_ref[...].astype(F32)
    v_all = v_ref[...].astype(F32)
    la = la_ref[...]
    la32 = la.astype(F32)
    w_pre = w0_ref[...] + _dot(jnp.tanh(la32).astype(BF16), wup_ref[...])
    lw_all = -jnp.exp(_log_sigmoid(w_pre) - 0.5)
    alr = _sigmoid(a0_ref[...] + _dot(la.astype(BF16), aup_ref[...]))
    gate_all = (_dot(_sigmoid(la32).astype(BF16), gupa_ref[...])
                + _dot(_sigmoid(lb_ref[...].astype(F32)).astype(BF16), gupb_ref[...]))
    if has_vres:
        mix = _sigmoid(v0_ref[...] + _dot(vl_ref[...].astype(BF16), vup_ref[...]))
        v_all = v_all + (vf_ref[...].astype(F32) - v_all) * mix
    kk = k_all * kk_ref[...]
    k2_all = k_all * (1.0 + (alr - 1.0) * ka_ref[...])
    sums = seg_sum(jnp.concatenate([kk * kk, r_all * k2_all * rk_ref[...]], axis=0))
    kk = kk * lax.rsqrt(jnp.maximum(sums[:tb], 1e-24))
    bonus_all = sums[tb:] * v_all
    a_all = -kk
    b_all = kk * alr

    chunks = range(RWKV_NCH)
    rows = [slice(ci * c, (ci + 1) * c) for ci in chunks]
    cl = [_dot_split2_lhs(tri_ref[...], lw_all[rw]) for rw in rows]
    pre = []
    for ci in chunks:
        rw = rows[ci]
        cl_last = cl[ci][c - 1:c, :]
        e_neg = jnp.exp(-cl[ci])
        e_end = jnp.exp(cl_last - cl[ci])
        b, k2, v = b_all[rw], k2_all[rw], v_all[rw]
        at = a_all[rw] * jnp.exp(cl[ci] - lw_all[rw])
        rt = r_all[rw] * jnp.exp(cl[ci])
        pre.append(dict(
            rt=rt, v=v, at=at, g_c=jnp.exp(cl_last),
            lhs=jnp.concatenate([at, rt], axis=0).astype(BF16),
            rhs=jnp.concatenate([block_diag(b * e_neg), block_diag(k2 * e_neg)], axis=0),
            hat=jnp.concatenate([b * e_end, k2 * e_end], axis=0).astype(BF16)))
    for p in pre:
        aa = _dot_nt(p["lhs"], p["rhs"])
        p.update(
            a_ab=jnp.where(strict, aa[:c, :gw], 0.0),
            a_ak=jnp.where(strict, aa[:c, gw:], 0.0).astype(BF16),
            p_rb=jnp.where(incl, aa[c:, :gw], 0.0).astype(BF16),
            p_rk=jnp.where(incl, aa[c:, gw:], 0.0).astype(BF16))

    a_bd = [block_diag(p["a_ab"]) for p in pre]
    tinv = [jnp.where(tt == jj, 1.0, jnp.where((tt >> 1) == (jj >> 1), p["a_ab"], 0.0)) for p in pre]
    for s in RWKV_INV_LEVELS:
        shift = int(np.log2(2 * s))
        off = ((tt >> shift) == (jj >> shift)) & ((tt & s) != 0) & ((jj & s) == 0)
        t_bd = [block_diag(t) for t in tinv]
        half = [jnp.where(off, _dot(tinv[ci].astype(BF16), a_bd[ci]), 0.0).astype(BF16) for ci in chunks]
        tinv = [tinv[ci] + _dot(half[ci], t_bd[ci]) for ci in chunks]

    t_b = [t.astype(BF16) for t in tinv]
    v_bd = [block_diag(p["v"]) for p in pre]
    wt = [_dot(t_b[ci], block_diag(pre[ci]["at"])) for ci in chunks]
    akv = [block_diag(_dot(pre[ci]["a_ak"], v_bd[ci])) for ci in chunks]
    u0 = [_dot(t_b[ci], akv[ci]) for ci in chunks]
    qh = [(pre[ci]["rt"] + _dot(pre[ci]["p_rb"], block_diag(wt[ci]))).astype(BF16) for ci in chunks]
    gmat = [jnp.where(same_head, _dot(wt[ci].T.astype(BF16), pre[ci]["hat"][:c]), 0.0).astype(BF16)
            for ci in chunks]
    y0 = [_dot(pre[ci]["p_rb"], block_diag(u0[ci])) + _dot(pre[ci]["p_rk"], v_bd[ci]) for ci in chunks]
    n0c = []
    for ci in chunks:
        uv_t = jnp.concatenate([u0[ci], pre[ci]["v"]], axis=0).T.astype(BF16)
        n0 = jnp.where(same_head, _dot(uv_t, pre[ci]["hat"]), 0.0)
        n0c.append(n0[0:c] + n0[c:2 * c] + n0[2 * c:3 * c] + n0[3 * c:4 * c])

    st = st_scr[...]
    ys = []
    for ci in chunks:
        ys.append(_dot_nt(qh[ci], block_diag(st)) + y0[ci])
        st = st * pre[ci]["g_c"] + _dot(st.astype(BF16), gmat[ci]) + n0c[ci]
    st_scr[...] = st

    y = jnp.concatenate(ys, axis=0)
    yc = y - seg_sum(y) * inv_n
    var = seg_sum(yc * yc) * inv_n
    yn = yc * lax.rsqrt(var + RWKV_GN_EPS) * gnw_ref[...] + gnb_ref[...]
    o_ref[...] = ((yn + bonus_all) * gate_all).astype(o_ref.dtype)


def _rwkv_call(z, z_first, vec, mats, tri, ones_bd, has_vres):
    t = z.shape[0]
    tb = CHUNK * RWKV_NCH
    gw = RWKV_GW
    groups = RWKV_HEADS // RWKV_GROUP
    col = lambda off: pl.BlockSpec((tb, gw), lambda p, c: (c, off // gw + p))
    fixed = lambda off, w: pl.BlockSpec((tb, w), lambda p, c: (c, off // w))
    rowv = pl.BlockSpec((1, gw), lambda p, c: (0, p))
    upm = lambda rows: pl.BlockSpec((rows, gw), lambda p, c: (0, p))
    const = lambda a: pl.BlockSpec(a.shape, lambda p, c: (0, 0))

    half = 2 * LANES
    in_specs = [col(C_RR), col(C_RK), col(C_RV), fixed(C_RL, half), fixed(C_RL + half, half)]
    args = [z, z, z, z, z]
    if has_vres:
        in_specs += [fixed(C_VRES, LANES), col(C_RV), upm(LANES), rowv]
        args += [z, z_first, mats["v_up"], vec["v0"]]
    in_specs += [rowv, upm(half), rowv, upm(half), upm(half), upm(half), rowv, rowv, rowv, rowv, rowv,
                 const(tri), const(ones_bd)]
    args += [vec["w0"], mats["w_up"], vec["a0"], mats["a_up"], mats["g_up_a"], mats["g_up_b"], vec["k_k"], vec["k_a"],
             vec["r_k"], vec["gn_w"], vec["gn_b"], tri, ones_bd]
    return pl.pallas_call(
        functools.partial(_rwkv_kernel, has_vres=has_vres),
        out_shape=jax.ShapeDtypeStruct((t, RWKV_WIDTH), BF16),
        grid=(groups, t // tb),
        in_specs=in_specs,
        out_specs=pl.BlockSpec((tb, gw), lambda p, c: (c, p)),
        scratch_shapes=[pltpu.VMEM((CHUNK, RWKV_GW), F32)],
        compiler_params=pltpu.CompilerParams(
            dimension_semantics=("arbitrary", "arbitrary"), vmem_limit_bytes=VMEM_LIMIT_BYTES),
        name="rwkv7_mixer",
    )(*args)


OUT_TM = 512
OUT_SUB = 256


def _outproj_kernel(x_ref, oa_ref, ob_ref, wa_ref, wb_ref, g_ref, gt_ref, gpre_ref, sh_ref, sc_ref, o_ref, h_ref):
    subs = [slice(s, s + OUT_SUB) for s in range(0, OUT_TM, OUT_SUB)]
    ys = [_dot(oa_ref[rw, :], wa_ref[...]) + _dot(ob_ref[rw, :], wb_ref[...]) for rw in subs]
    post_scale = gt_ref[0] * g_ref[...]
    pre_scale = gpre_ref[...] * (1.0 + sc_ref[0])
    for rw, y in zip(subs, ys):
        ms = jnp.mean(y * y, axis=-1, keepdims=True)
        xn = x_ref[rw, :] + (y * lax.rsqrt(ms + RMS_EPS)) * post_scale
        o_ref[rw, :] = xn
        ms2 = jnp.mean(xn * xn, axis=-1, keepdims=True)
        h_ref[rw, :] = ((xn * lax.rsqrt(ms2 + RMS_EPS)) * pre_scale + sh_ref[0]).astype(h_ref.dtype)


def _outproj_call(x2, o_gla, o_rwkv, w_a, w_b, g_row, gpre_row, mod, layer):
    t, d = x2.shape
    return pl.pallas_call(
        _outproj_kernel,
        out_shape=(jax.ShapeDtypeStruct((t, d), F32), jax.ShapeDtypeStruct((t, d), BF16)),
        grid=(t // OUT_TM,),
        in_specs=[
            pl.BlockSpec((OUT_TM, d), lambda i: (i, 0)),
            pl.BlockSpec((OUT_TM, GLA_V), lambda i: (i, 0)),
            pl.BlockSpec((OUT_TM, RWKV_WIDTH), lambda i: (i, 0)),
            pl.BlockSpec((GLA_V, d), lambda i: (0, 0)),
            pl.BlockSpec((RWKV_WIDTH, d), lambda i: (0, 0)),
            pl.BlockSpec((1, d), lambda i: (0, 0)),
            pl.BlockSpec((1, 1, d), lambda i: (layer, 0, 2)),
            pl.BlockSpec((1, d), lambda i: (0, 0)),
            pl.BlockSpec((1, 1, d), lambda i: (layer, 0, 3)),
            pl.BlockSpec((1, 1, d), lambda i: (layer, 0, 4)),
        ],
        out_specs=(pl.BlockSpec((OUT_TM, d), lambda i: (i, 0)), pl.BlockSpec((OUT_TM, d), lambda i: (i, 0))),
        compiler_params=pltpu.CompilerParams(
            dimension_semantics=("arbitrary",), vmem_limit_bytes=VMEM_LIMIT_BYTES),
        name="outproj",
    )(x2, o_gla, o_rwkv, w_a, w_b, g_row, mod, gpre_row, mod, mod)


FFN_TM = 512
FFN_TF = 1024


def _ffn_kernel(x_ref, h_ref, w1_ref, w2_ref, gpost_ref, gt_ref, o_ref, acc_scr):
    f = pl.program_id(1)

    @pl.when(f == 0)
    def _():
        acc_scr[...] = jnp.zeros(acc_scr.shape, F32)

    u = jnp.maximum(_dot(h_ref[...], w1_ref[...]), 0.0)
    acc_scr[...] += _dot((u * u).astype(BF16), w2_ref[...])

    @pl.when(f == pl.num_programs(1) - 1)
    def _():
        y = acc_scr[...]
        ms = jnp.mean(y * y, axis=-1, keepdims=True)
        o_ref[...] = x_ref[...] + gt_ref[0] * (y * lax.rsqrt(ms + RMS_EPS) * gpost_ref[...])


def _ffn_call(x2, h2, gpost_row, mod, layer, w1, w2):
    t, d = x2.shape
    dff = w1.shape[1]
    return pl.pallas_call(
        _ffn_kernel,
        out_shape=jax.ShapeDtypeStruct((t, d), F32),
        grid=(t // FFN_TM, dff // FFN_TF),
        in_specs=[
            pl.BlockSpec((FFN_TM, d), lambda i, f: (i, 0)),
            pl.BlockSpec((FFN_TM, d), lambda i, f: (i, 0)),
            pl.BlockSpec((d, FFN_TF), lambda i, f: (0, f)),
            pl.BlockSpec((FFN_TF, d), lambda i, f: (f, 0)),
            pl.BlockSpec((1, d), lambda i, f: (0, 0)),
            pl.BlockSpec((1, 1, d), lambda i, f: (layer, 0, 5)),
        ],
        out_specs=pl.BlockSpec((FFN_TM, d), lambda i, f: (i, 0)),
        scratch_shapes=[pltpu.VMEM((FFN_TM, d), F32)],
        compiler_params=pltpu.CompilerParams(
            dimension_semantics=("arbitrary", "arbitrary"), vmem_limit_bytes=VMEM_LIMIT_BYTES),
        name="ffn",
    )(x2, h2, w1, w2, gpost_row, mod)


def _pad_cols(w, width):
    return jnp.pad(w, ((0, 0), (0, width - w.shape[1])))


def _pad_rows(w, rows):
    return jnp.pad(w, ((0, rows - w.shape[0]), (0, 0)))


def _pack_inproj(w_in_t, mu_rwkv, vres_w_down_t, vres_mu):
    d = w_in_t.shape[1]
    n_gla = C_GA + GLA_GATE_RANK
    if vres_w_down_t is None:
        vres_w_down_t = jnp.zeros((RWKV_V_RANK, d), w_in_t.dtype)
        vres_mu = jnp.zeros((RWKV_V_RANK,), F32)
    zrows = lambda n: jnp.zeros((n, d), w_in_t.dtype)
    n_rwkv = w_in_t.shape[0] - n_gla
    w_pack_t = jnp.concatenate(
        [w_in_t[:n_gla], zrows(C_VRES - n_gla), vres_w_down_t, zrows(C_RR - C_VRES - RWKV_V_RANK),
         w_in_t[n_gla:], zrows(N_PACK - C_RR - n_rwkv)], axis=0)
    mu_pack = jnp.concatenate(
        [jnp.zeros((1, C_VRES), F32), _pad_cols(vres_mu[None, :], C_RR - C_VRES),
         _pad_cols(mu_rwkv[None, :], N_PACK - C_RR)], axis=1)
    return w_pack_t, mu_pack


def kernel(x, c, w_ada, b_ada, g_pre_mix, g_post_mix, g_pre_ffn, g_post_ffn, w_in, gla_w_a_up, gla_b_a, gla_norm_w, rwkv_mu, rwkv_w0, rwkv_w_up, rwkv_a0, rwkv_a_up, rwkv_g_up, rwkv_k_k, rwkv_k_a, rwkv_r_k, rwkv_gn_w, rwkv_gn_b, vres_w_down, vres_mu, vres_up, vres_v0, w_out, w_ff1, w_ff2):
    bsz, t, d = x.shape
    assert bsz == 1 and d == D_MODEL and t % (CHUNK * max(GLA_NCH, RWKV_NCH)) == 0
    assert t % FFN_TM == 0 and t % INP_TM == 0 and t % OUT_TM == 0
    n_layers = w_ada.shape[0]

    mod = _ada_call(c.reshape(d, 1), w_ada, b_ada.reshape(n_layers, 1, 6 * d))

    p_gla = jnp.asarray(_gla_exponent_matrix(), BF16)
    gla_masks = tuple(jnp.asarray(m, BF16) for m in _gla_masks())
    tri = jnp.asarray(np.tril(np.ones((CHUNK, CHUNK), np.float32)), BF16)
    head_of_lane = np.arange(RWKV_GW) // RWKV_HEAD
    ones_bd = jnp.asarray((head_of_lane[:, None] == head_of_lane[None, :]).astype(np.float32), BF16)

    w_in_t = jnp.swapaxes(w_in, 1, 2).astype(BF16)
    x2 = x.reshape(t, d)
    z_first = None
    for i in range(n_layers):
        j = i - 1
        w_pack_t, mu_pack = _pack_inproj(
            w_in_t[i], rwkv_mu[i],
            vres_w_down[j].T.astype(BF16) if i > 0 else None, vres_mu[j] if i > 0 else None)
        z = _inproj_call(x2, g_pre_mix[i][None, :], mod, i, w_pack_t, mu_pack)
        if i == 0:
            z_first = z

        o_gla = _gla_call(
            z, _pad_rows(gla_w_a_up[i], LANES).astype(BF16), gla_b_a[i][None, :], gla_norm_w[i][None, :], p_gla,
            gla_masks)

        row = lambda a: a.reshape(1, RWKV_WIDTH)
        vec = dict(w0=row(rwkv_w0[i]), a0=row(rwkv_a0[i]), k_k=row(rwkv_k_k[i]), k_a=row(rwkv_k_a[i]),
                   r_k=row(rwkv_r_k[i]), gn_w=row(rwkv_gn_w[i]), gn_b=row(rwkv_gn_b[i]))
        half = 2 * LANES
        place = lambda w, start: jnp.pad(w, ((start, half - start - w.shape[0]), (0, 0))).astype(BF16)
        g_split = half - LORA_G[0]
        mats = dict(w_up=place(rwkv_w_up[i], LORA_W[0]), a_up=place(rwkv_a_up[i], LORA_A[0]),
                    g_up_a=place(rwkv_g_up[i][:g_split], LORA_G[0]), g_up_b=place(rwkv_g_up[i][g_split:], 0))
        if i > 0:
            vec["v0"] = row(vres_v0[j])
            mats["v_up"] = _pad_rows(vres_up[j], LANES).astype(BF16)
        o_rwkv = _rwkv_call(z, z_first, vec, mats, tri, ones_bd, has_vres=i > 0)

        w_o = w_out[i].astype(BF16)
        x2, h2 = _outproj_call(x2, o_gla, o_rwkv, w_o[:GLA_V], w_o[GLA_V:], g_post_mix[i][None, :],
                               g_pre_ffn[i][None, :], mod, i)
        x2 = _ffn_call(x2, h2, g_post_ffn[i][None, :], mod, i, w_ff1[i].astype(BF16), w_ff2[i].astype(BF16))
    return x2.reshape(bsz, t, d)
```

```python
import functools

import numpy as np
import jax
import jax.numpy as jnp
from jax import lax
from jax.experimental import pallas as pl
from jax.experimental.pallas import tpu as pltpu

F32 = jnp.float32
BF16 = jnp.bfloat16

D_MODEL = 2048
DEPTH = 2
GLA_V = 1024
GLA_DV = 128
GLA_HEADS = 8
GLA_DK = 64
GLA_QK = 512
GLA_GATE_RANK = 16
GLA_TAU = 16.0
RWKV_WIDTH = 1024
RWKV_HEAD = 64
RWKV_HEADS = 16
RWKV_W_RANK = 96
RWKV_A_RANK = 96
RWKV_G_RANK = 256
RWKV_V_RANK = 64
RWKV_GN_EPS = 64e-5
D_FF = 4 * D_MODEL
RMS_EPS = 1e-6

LANES = 128
CHUNK = 64
VMEM_LIMIT_BYTES = 56 * 1024 * 1024

C_GQ = 0
C_GK = 512
C_GV = 1024
C_GG = 2048
C_GA = 3072
C_VRES = 3200
C_RR = 3328
C_RK = 4352
C_RV = 5376
C_RG = 6400
C_RW = 6656
C_RA = 6784
N_PACK = 6912


def _dot(a, b):
    return jnp.dot(a, b, preferred_element_type=F32)


def _dot_nt(a, b):
    return lax.dot_general(a, b, (((1,), (1,)), ((), ())), preferred_element_type=F32)


def _dot_split2_lhs(p, x):
    hi = x.astype(BF16)
    lo = (x - hi.astype(F32)).astype(BF16)
    y = _dot(p, jnp.concatenate([hi, lo], axis=1))
    return y[:, :x.shape[1]] + y[:, x.shape[1]:]


def _dot_split2_rhs(x, p):
    hi = x.astype(BF16)
    lo = (x - hi.astype(F32)).astype(BF16)
    y = _dot(jnp.concatenate([hi, lo], axis=0), p)
    return y[:x.shape[0]] + y[x.shape[0]:]


def _log_sigmoid(x):
    return jnp.minimum(x, 0.0) - jnp.log(1.0 + jnp.exp(-jnp.abs(x)))


def _sigmoid(x):
    return 1.0 / (1.0 + jnp.exp(-x))


ADA_TN = 1024


def _ada_kernel(c_ref, w_ref, b_ref, o_ref):
    c = c_ref[...]
    cond = c * _sigmoid(c)
    o_ref[0] = jnp.sum(w_ref[0] * cond, axis=0, keepdims=True) + b_ref[0]


def _ada_call(c_col, w_ada, b_ada3):
    n_layers, d, n = w_ada.shape
    return pl.pallas_call(
        _ada_kernel,
        out_shape=jax.ShapeDtypeStruct((n_layers, 1, n), F32),
        grid=(n_layers, n // ADA_TN),
        in_specs=[
            pl.BlockSpec((d, 1), lambda l, j: (0, 0)),
            pl.BlockSpec((1, d, ADA_TN), lambda l, j: (l, 0, j)),
            pl.BlockSpec((1, 1, ADA_TN), lambda l, j: (l, 0, j)),
        ],
        out_specs=pl.BlockSpec((1, 1, ADA_TN), lambda l, j: (l, 0, j)),
        compiler_params=pltpu.CompilerParams(
            dimension_semantics=("arbitrary", "arbitrary"), vmem_limit_bytes=VMEM_LIMIT_BYTES),
        name="ada_mod",
    )(c_col, w_ada, b_ada3)


INP_TM = 1024
INP_TN = 768


def _inproj_kernel(x_ref, g_ref, sh_ref, sc_ref, w_ref, mu_ref, z_ref, h_scr, carry_scr):
    i = pl.program_id(0)
    j = pl.program_id(1)

    @pl.when(j == 0)
    def _():
        x = x_ref[...]
        ms = jnp.mean(x * x, axis=-1, keepdims=True)
        y = x * lax.rsqrt(ms + RMS_EPS) * g_ref[...]
        h_scr[...] = (y * (1.0 + sc_ref[0]) + sh_ref[0]).astype(BF16)

    @pl.when(i == 0)
    def _():
        carry_scr[j] = jnp.zeros(carry_scr.shape[1:], F32)

    z = _dot_nt(h_scr[...], w_ref[...])
    prev = carry_scr[j]
    row = lax.broadcasted_iota(jnp.int32, z.shape, 0)
    shifted = jnp.where(row == 0, prev, pltpu.roll(z, 1, 0))
    carry_scr[j] = z[z.shape[0] - 1:, :]
    z_ref[...] = (z + mu_ref[...] * (shifted - z)).astype(z_ref.dtype)


def _inproj_call(x2, g_row, mod, layer, w_pack_t, mu_pack):
    t, d = x2.shape
    n = w_pack_t.shape[0]
    nj = n // INP_TN
    return pl.pallas_call(
        _inproj_kernel,
        out_shape=jax.ShapeDtypeStruct((t, n), BF16),
        grid=(t // INP_TM, nj),
        in_specs=[
            pl.BlockSpec((INP_TM, d), lambda i, j: (i, 0)),
            pl.BlockSpec((1, d), lambda i, j: (0, 0)),
            pl.BlockSpec((1, 1, d), lambda i, j: (layer, 0, 0)),
            pl.BlockSpec((1, 1, d), lambda i, j: (layer, 0, 1)),
            pl.BlockSpec((INP_TN, d), lambda i, j: (j, 0)),
            pl.BlockSpec((1, INP_TN), lambda i, j: (0, j)),
        ],
        out_specs=pl.BlockSpec((INP_TM, INP_TN), lambda i, j: (i, j)),
        scratch_shapes=[pltpu.VMEM((INP_TM, d), BF16), pltpu.VMEM((nj, 1, INP_TN), F32)],
        compiler_params=pltpu.CompilerParams(
            dimension_semantics=("arbitrary", "arbitrary"), vmem_limit_bytes=VMEM_LIMIT_BYTES),
        name="inproj",
    )(x2, g_row, mod, mod, w_pack_t, mu_pack)


GLA_NCH = 16
GLA_GROUP = 4
GLA_QW = GLA_GROUP * GLA_DK
GLA_VW = GLA_GROUP * GLA_DV
GLA_LEVELS = (32, 16, 8, 4, 2, 1)


def _gla_exponent_matrix():
    c = CHUNK
    p = np.zeros((2 * c + len(GLA_LEVELS) * c, c), np.float32)
    for i in range(c):
        p[i, : i + 1] = 1.0
        p[c + i, i + 1:] = 1.0
    for li, s in enumerate(GLA_LEVELS):
        base = 2 * c + li * c
        for i in range(c):
            m = (i // (2 * s)) * (2 * s) + s
            if i & s:
                p[base + i, m + 1: i + 1] = 1.0
            else:
                p[base + i, i + 1: m + 1] = 1.0
    return p


def _gla_masks():
    hq = np.arange(GLA_QW) // GLA_DK
    hv = np.arange(GLA_VW) // GLA_DV
    qq = (hq[:, None] == hq[None, :]).astype(np.float32)
    qv = (hq[:, None] == hv[None, :]).astype(np.float32)
    return qq, qv, qv.T.copy()


def _gla_kernel(q_ref, k_ref, v_ref, g_ref, al_ref, wup_ref, ba_ref, nw_ref, p_ref, mqq_ref, mqv_ref, mvq_ref,
                o_ref, st_scr):
    @pl.when(pl.program_id(1) == 0)
    def _():
        st_scr[...] = jnp.zeros(st_scr.shape, F32)

    c = CHUNK
    qw, vw = GLA_QW, GLA_VW
    tt = lax.broadcasted_iota(jnp.int32, (c, qw), 0)
    jj = lax.broadcasted_iota(jnp.int32, (c, qw), 1) & (GLA_DK - 1)
    mqq, mqv, mvq = mqq_ref[...], mqv_ref[...], mvq_ref[...]
    same_head_vq = ((lax.broadcasted_iota(jnp.int32, (vw, qw), 0) >> int(np.log2(GLA_DV)))
                    == (lax.broadcasted_iota(jnp.int32, (vw, qw), 1) >> int(np.log2(GLA_DK))))

    def bd_qk(x):
        return jnp.concatenate([x.astype(BF16)] * GLA_GROUP, axis=0) * mqq

    x = _dot(al_ref[...].astype(BF16), wup_ref[...]) + ba_ref[...]
    la_all = _log_sigmoid(x) * (1.0 / GLA_TAU)
    q_all = q_ref[...].astype(F32) * (GLA_DK ** -0.5)
    k_all = k_ref[...].astype(F32)
    v_all = v_ref[...].astype(F32)

    chunks = range(GLA_NCH)
    rows = [slice(ci * c, (ci + 1) * c) for ci in chunks]
    e_all = [_dot_split2_lhs(p_ref[...], la_all[rw]) for rw in rows]
    q = [q_all[rw] for rw in rows]
    k = [k_all[rw] for rw in rows]

    qk_sum = _dot_split2_rhs(q_all * k_all, mqq)
    scores = [jnp.where(tt == jj, qk_sum[rw], 0.0) for rw in rows]
    for li, s in enumerate(GLA_LEVELS):
        second = (tt & s) != 0
        shift = int(np.log2(2 * s))
        same = (tt >> shift) == (jj >> shift)
        e = [jnp.exp(e_all[ci][(2 + li) * c:(3 + li) * c]) for ci in chunks]
        qd = [jnp.where(second, q[ci] * e[ci], 0.0).astype(BF16) for ci in chunks]
        kd = [bd_qk(jnp.where(second, 0.0, k[ci] * e[ci])) for ci in chunks]
        scores = [scores[ci] + jnp.where(same, _dot_nt(qd[ci], kd[ci]), 0.0) for ci in chunks]

    v_bd = [jnp.concatenate([v_all[rw].astype(BF16)] * GLA_GROUP, axis=0) * mqv for rw in rows]
    o_intra = [_dot(scores[ci].astype(BF16), v_bd[ci]) for ci in chunks]
    upd = []
    for ci in chunks:
        ke = (k[ci] * jnp.exp(e_all[ci][c:2 * c])).astype(BF16)
        m = jnp.where(same_head_vq, _dot(v_all[rows[ci]].T.astype(BF16), ke), 0.0)
        dv = GLA_DV
        upd.append(m[0:dv] + m[dv:2 * dv] + m[2 * dv:3 * dv] + m[3 * dv:4 * dv])

    st = st_scr[...]
    os_ = []
    for ci in chunks:
        b = e_all[ci][0:c]
        st_bd = jnp.concatenate([st.astype(BF16)] * GLA_GROUP, axis=0) * mvq
        os_.append(_dot_nt((q[ci] * jnp.exp(b)).astype(BF16), st_bd) + o_intra[ci])
        st = st * jnp.exp(b[c - 1:c, :]) + upd[ci]
    st_scr[...] = st

    o = jnp.concatenate(os_, axis=0)
    g = g_ref[...].astype(F32)
    gs = g * _sigmoid(g) * nw_ref[...]
    outs = []
    for h in range(GLA_GROUP):
        oh = o[:, h * GLA_DV:(h + 1) * GLA_DV]
        oh = oh * lax.rsqrt(jnp.mean(oh * oh, axis=-1, keepdims=True) + RMS_EPS)
        outs.append(oh * gs[:, h * GLA_DV:(h + 1) * GLA_DV])
    o_ref[...] = jnp.concatenate(outs, axis=1).astype(o_ref.dtype)


def _gla_call(z, wup_pad, ba_row, nw_row, p_mat, masks):
    t = z.shape[0]
    tb = CHUNK * GLA_NCH
    groups = GLA_HEADS // GLA_GROUP
    blk = lambda w, off: pl.BlockSpec((tb, w), lambda p, c: (c, off // w + p))
    const = lambda a: pl.BlockSpec(a.shape, lambda p, c: (0, 0))
    return pl.pallas_call(
        _gla_kernel,
        out_shape=jax.ShapeDtypeStruct((t, GLA_V), BF16),
        grid=(groups, t // tb),
        in_specs=[
            blk(GLA_QW, C_GQ), blk(GLA_QW, C_GK), blk(GLA_VW, C_GV), blk(GLA_VW, C_GG),
            pl.BlockSpec((tb, LANES), lambda p, c: (c, C_GA // LANES)),
            pl.BlockSpec((LANES, GLA_QW), lambda p, c: (0, p)),
            pl.BlockSpec((1, GLA_QW), lambda p, c: (0, p)),
            pl.BlockSpec((1, GLA_VW), lambda p, c: (0, p)),
            const(p_mat), const(masks[0]), const(masks[1]), const(masks[2]),
        ],
        out_specs=pl.BlockSpec((tb, GLA_VW), lambda p, c: (c, p)),
        scratch_shapes=[pltpu.VMEM((GLA_DV, GLA_QW), F32)],
        compiler_params=pltpu.CompilerParams(
            dimension_semantics=("arbitrary", "arbitrary"), vmem_limit_bytes=VMEM_LIMIT_BYTES),
        name="gla_mixer",
    )(z, z, z, z, z, wup_pad, ba_row, nw_row, p_mat, *masks)


RWKV_NCH = 16
RWKV_GROUP = 4
RWKV_GW = RWKV_GROUP * RWKV_HEAD
RWKV_INV_LEVELS = (2, 4, 8, 16, 32)


def _rwkv_kernel(*refs, has_vres):
    if has_vres:
        (r_ref, k_ref, v_ref, wl_ref, al_ref, gl_ref, vl_ref, vf_ref, vup_ref, v0_ref,
         w0_ref, wup_ref, a0_ref, aup_ref, gup_ref, kk_ref, ka_ref, rk_ref, gnw_ref, gnb_ref,
         tri_ref, ones_ref, o_ref, st_scr) = refs
    else:
        (r_ref, k_ref, v_ref, wl_ref, al_ref, gl_ref,
         w0_ref, wup_ref, a0_ref, aup_ref, gup_ref, kk_ref, ka_ref, rk_ref, gnw_ref, gnb_ref,
         tri_ref, ones_ref, o_ref, st_scr) = refs

    @pl.when(pl.program_id(1) == 0)
    def _():
        st_scr[...] = jnp.zeros(st_scr.shape, F32)

    c = CHUNK
    gw = RWKV_GW
    tt = lax.broadcasted_iota(jnp.int32, (c, gw), 0)
    jj = lax.broadcasted_iota(jnp.int32, (c, gw), 1) & (RWKV_HEAD - 1)
    strict = tt > jj
    incl = tt >= jj
    ones_bd = ones_ref[...]
    same_head_b = ones_bd != 0
    hshift = int(np.log2(RWKV_HEAD))
    same_head = ((lax.broadcasted_iota(jnp.int32, (gw, gw), 0) >> hshift)
                 == (lax.broadcasted_iota(jnp.int32, (gw, gw), 1) >> hshift))
    inv_n = 1.0 / RWKV_HEAD

    def seg_sum(x):
        return _dot_split2_rhs(x, ones_bd)

    def block_diag(x):
        return jnp.concatenate([x.astype(BF16)] * RWKV_GROUP, axis=0) * ones_bd

    tb = c * RWKV_NCH
    r_all = r_ref[...].astype(F32)
    k_all = k_ref[...].astype(F32)
    v_all = v_ref[...].astype(F32)
    w_pre = w0_ref[...] + _dot(jnp.tanh(wl_ref[...].astype(F32)).astype(BF16), wup_ref[...])
    lw_all = -jnp.exp(_log_sigmoid(w_pre) - 0.5)
    alr = _sigmoid(a0_ref[...] + _dot(al_ref[...].astype(BF16), aup_ref[...]))
    gate_all = _dot(_sigmoid(gl_ref[...].astype(F32)).astype(BF16), gup_ref[...])
    if has_vres:
        mix = _sigmoid(v0_ref[...] + _dot(vl_ref[...].astype(BF16), vup_ref[...]))
        v_all = v_all + (vf_ref[...].astype(F32) - v_all) * mix
    kk = k_all * kk_ref[...]
    k2_all = k_all * (1.0 + (alr - 1.0) * ka_ref[...])
    sums = seg_sum(jnp.concatenate([kk * kk, r_all * k2_all * rk_ref[...]], axis=0))
    kk = kk * lax.rsqrt(jnp.maximum(sums[:tb], 1e-24))
    bonus_all = sums[tb:] * v_all
    a_all = -kk
    b_all = kk * alr

    chunks = range(RWKV_NCH)
    rows = [slice(ci * c, (ci + 1) * c) for ci in chunks]
    cl = [_dot_split2_lhs(tri_ref[...], lw_all[rw]) for rw in rows]
    pre = []
    for ci in chunks:
        rw = rows[ci]
        cl_last = cl[ci][c - 1:c, :]
        e_neg = jnp.exp(-cl[ci])
        e_end = jnp.exp(cl_last - cl[ci])
        b, k2, v = b_all[rw], k2_all[rw], v_all[rw]
        at = a_all[rw] * jnp.exp(cl[ci] - lw_all[rw])
        rt = r_all[rw] * jnp.exp(cl[ci])
        pre.append(dict(
            rt=rt, v=v, at=at, g_c=jnp.exp(cl_last),
            lhs=jnp.concatenate([at, rt], axis=0).astype(BF16),
            rhs=jnp.concatenate([block_diag(b * e_neg), block_diag(k2 * e_neg)], axis=0),
            hat=jnp.concatenate([b * e_end, k2 * e_end], axis=0).astype(BF16)))
    for p in pre:
        aa = _dot_nt(p["lhs"], p["rhs"])
        p.update(
            a_ab=jnp.where(strict, aa[:c, :gw], 0.0),
            a_ak=jnp.where(strict, aa[:c, gw:], 0.0).astype(BF16),
            p_rb=jnp.where(incl, aa[c:, :gw], 0.0).astype(BF16),
            p_rk=jnp.where(incl, aa[c:, gw:], 0.0).astype(BF16))

    a_bd = [block_diag(p["a_ab"]) for p in pre]
    tinv = [jnp.where(tt == jj, 1.0, jnp.where((tt >> 1) == (jj >> 1), p["a_ab"], 0.0)) for p in pre]
    for s in RWKV_INV_LEVELS:
        shift = int(np.log2(2 * s))
        off = ((tt >> shift) == (jj >> shift)) & ((tt & s) != 0) & ((jj & s) == 0)
        t_bd = [block_diag(t) for t in tinv]
        half = [jnp.where(off, _dot(tinv[ci].astype(BF16), a_bd[ci]), 0.0).astype(BF16) for ci in chunks]
        tinv = [tinv[ci] + _dot(half[ci], t_bd[ci]) for ci in chunks]

    t_b = [t.astype(BF16) for t in tinv]
    v_bd = [block_diag(p["v"]) for p in pre]
    wt = [_dot(t_b[ci], block_diag(pre[ci]["at"])) for ci in chunks]
    akv = [block_diag(_dot(pre[ci]["a_ak"], v_bd[ci])) for ci in chunks]
    u0 = [_dot(t_b[ci], akv[ci]) for ci in chunks]
    qh = [(pre[ci]["rt"] + _dot(pre[ci]["p_rb"], block_diag(wt[ci]))).astype(BF16) for ci in chunks]
    gmat = [jnp.where(same_head, _dot(wt[ci].T.astype(BF16), pre[ci]["hat"][:c]), 0.0).astype(BF16)
            for ci in chunks]
    y0 = [_dot(pre[ci]["p_rb"], block_diag(u0[ci])) + _dot(pre[ci]["p_rk"], v_bd[ci]) for ci in chunks]
    n0c = []
    for ci in chunks:
        uv_t = jnp.concatenate([u0[ci], pre[ci]["v"]], axis=0).T.astype(BF16)
        n0 = jnp.where(same_head, _dot(uv_t, pre[ci]["hat"]), 0.0)
        n0c.append(n0[0:c] + n0[c:2 * c] + n0[2 * c:3 * c] + n0[3 * c:4 * c])

    st = st_scr[...]
    ys = []
    for ci in chunks:
        ys.append(_dot_nt(qh[ci], block_diag(st)) + y0[ci])
        st = st * pre[ci]["g_c"] + _dot(st.astype(BF16), gmat[ci]) + n0c[ci]
    st_scr[...] = st

    y = jnp.concatenate(ys, axis=0)
    yc = y - seg_sum(y) * inv_n
    var = seg_sum(yc * yc) * inv_n
    yn = yc * lax.rsqrt(var + RWKV_GN_EPS) * gnw_ref[...] + gnb_ref[...]
    o_ref[...] = ((yn + bonus_all) * gate_all).astype(o_ref.dtype)


def _rwkv_call(z, z_first, vec, mats, tri, ones_bd, has_vres):
    t = z.shape[0]
    tb = CHUNK * RWKV_NCH
    gw = RWKV_GW
    groups = RWKV_HEADS // RWKV_GROUP
    col = lambda off: pl.BlockSpec((tb, gw), lambda p, c: (c, off // gw + p))
    fixed = lambda off, w: pl.BlockSpec((tb, w), lambda p, c: (c, off // w))
    rowv = pl.BlockSpec((1, gw), lambda p, c: (0, p))
    upm = lambda rows: pl.BlockSpec((rows, gw), lambda p, c: (0, p))
    const = lambda a: pl.BlockSpec(a.shape, lambda p, c: (0, 0))

    in_specs = [col(C_RR), col(C_RK), col(C_RV), fixed(C_RW, LANES), fixed(C_RA, LANES), fixed(C_RG, 2 * LANES)]
    args = [z, z, z, z, z, z]
    if has_vres:
        in_specs += [fixed(C_VRES, LANES), col(C_RV), upm(LANES), rowv]
        args += [z, z_first, mats["v_up"], vec["v0"]]
    in_specs += [rowv, upm(LANES), rowv, upm(LANES), upm(2 * LANES), rowv, rowv, rowv, rowv, rowv,
                 const(tri), const(ones_bd)]
    args += [vec["w0"], mats["w_up"], vec["a0"], mats["a_up"], mats["g_up"], vec["k_k"], vec["k_a"],
             vec["r_k"], vec["gn_w"], vec["gn_b"], tri, ones_bd]
    return pl.pallas_call(
        functools.partial(_rwkv_kernel, has_vres=has_vres),
        out_shape=jax.ShapeDtypeStruct((t, RWKV_WIDTH), BF16),
        grid=(groups, t // tb),
        in_specs=in_specs,
        out_specs=pl.BlockSpec((tb, gw), lambda p, c: (c, p)),
        scratch_shapes=[pltpu.VMEM((CHUNK, RWKV_GW), F32)],
        compiler_params=pltpu.CompilerParams(
            dimension_semantics=("arbitrary", "arbitrary"), vmem_limit_bytes=VMEM_LIMIT_BYTES),
        name="rwkv7_mixer",
    )(*args)


OUT_TM = 512
OUT_SUB = 256


def _outproj_kernel(x_ref, oa_ref, ob_ref, wa_ref, wb_ref, g_ref, gt_ref, gpre_ref, sh_ref, sc_ref, o_ref, h_ref):
    subs = [slice(s, s + OUT_SUB) for s in range(0, OUT_TM, OUT_SUB)]
    ys = [_dot(oa_ref[rw, :], wa_ref[...]) + _dot(ob_ref[rw, :], wb_ref[...]) for rw in subs]
    post_scale = gt_ref[0] * g_ref[...]
    pre_scale = gpre_ref[...] * (1.0 + sc_ref[0])
    for rw, y in zip(subs, ys):
        ms = jnp.mean(y * y, axis=-1, keepdims=True)
        xn = x_ref[rw, :] + (y * lax.rsqrt(ms + RMS_EPS)) * post_scale
        o_ref[rw, :] = xn
        ms2 = jnp.mean(xn * xn, axis=-1, keepdims=True)
        h_ref[rw, :] = ((xn * lax.rsqrt(ms2 + RMS_EPS)) * pre_scale + sh_ref[0]).astype(h_ref.dtype)


def _outproj_call(x2, o_gla, o_rwkv, w_out_all, g_row, gpre_row, mod, layer):
    t, d = x2.shape
    assert GLA_V == RWKV_WIDTH
    return pl.pallas_call(
        _outproj_kernel,
        out_shape=(jax.ShapeDtypeStruct((t, d), F32), jax.ShapeDtypeStruct((t, d), BF16)),
        grid=(t // OUT_TM,),
        in_specs=[
            pl.BlockSpec((OUT_TM, d), lambda i: (i, 0)),
            pl.BlockSpec((OUT_TM, GLA_V), lambda i: (i, 0)),
            pl.BlockSpec((OUT_TM, RWKV_WIDTH), lambda i: (i, 0)),
            pl.BlockSpec((None, GLA_V, d), lambda i: (layer, 0, 0)),
            pl.BlockSpec((None, RWKV_WIDTH, d), lambda i: (layer, 1, 0)),
            pl.BlockSpec((1, d), lambda i: (0, 0)),
            pl.BlockSpec((1, 1, d), lambda i: (layer, 0, 2)),
            pl.BlockSpec((1, d), lambda i: (0, 0)),
            pl.BlockSpec((1, 1, d), lambda i: (layer, 0, 3)),
            pl.BlockSpec((1, 1, d), lambda i: (layer, 0, 4)),
        ],
        out_specs=(pl.BlockSpec((OUT_TM, d), lambda i: (i, 0)), pl.BlockSpec((OUT_TM, d), lambda i: (i, 0))),
        compiler_params=pltpu.CompilerParams(
            dimension_semantics=("arbitrary",), vmem_limit_bytes=VMEM_LIMIT_BYTES),
        name="outproj",
    )(x2, o_gla, o_rwkv, w_out_all, w_out_all, g_row, mod, gpre_row, mod, mod)


FFN_TM = 512
FFN_TF = 1024


def _ffn_kernel(x_ref, h_ref, w1_ref, w2_ref, gpost_ref, gt_ref, o_ref, acc_scr):
    f = pl.program_id(1)

    @pl.when(f == 0)
    def _():
        acc_scr[...] = jnp.zeros(acc_scr.shape, F32)

    u = jnp.maximum(_dot(h_ref[...], w1_ref[...]), 0.0)
    acc_scr[...] += _dot((u * u).astype(BF16), w2_ref[...])

    @pl.when(f == pl.num_programs(1) - 1)
    def _():
        y = acc_scr[...]
        ms = jnp.mean(y * y, axis=-1, keepdims=True)
        o_ref[...] = x_ref[...] + gt_ref[0] * (y * lax.rsqrt(ms + RMS_EPS) * gpost_ref[...])


def _ffn_call(x2, h2, gpost_row, mod, layer, w1, w2):
    t, d = x2.shape
    dff = w1.shape[2]
    return pl.pallas_call(
        _ffn_kernel,
        out_shape=jax.ShapeDtypeStruct((t, d), F32),
        grid=(t // FFN_TM, dff // FFN_TF),
        in_specs=[
            pl.BlockSpec((FFN_TM, d), lambda i, f: (i, 0)),
            pl.BlockSpec((FFN_TM, d), lambda i, f: (i, 0)),
            pl.BlockSpec((None, d, FFN_TF), lambda i, f: (layer, 0, f)),
            pl.BlockSpec((None, FFN_TF, d), lambda i, f: (layer, f, 0)),
            pl.BlockSpec((1, d), lambda i, f: (0, 0)),
            pl.BlockSpec((1, 1, d), lambda i, f: (layer, 0, 5)),
        ],
        out_specs=pl.BlockSpec((FFN_TM, d), lambda i, f: (i, 0)),
        scratch_shapes=[pltpu.VMEM((FFN_TM, d), F32)],
        compiler_params=pltpu.CompilerParams(
            dimension_semantics=("arbitrary", "arbitrary"), vmem_limit_bytes=VMEM_LIMIT_BYTES),
        name="ffn",
    )(x2, h2, w1, w2, gpost_row, mod)


def _pad_cols(w, width):
    return jnp.pad(w, ((0, 0), (0, width - w.shape[1])))


def _pad_rows(w, rows):
    return jnp.pad(w, ((0, rows - w.shape[0]), (0, 0)))


def _pack_inproj(w_in_t, mu_rwkv, vres_w_down_t, vres_mu):
    d = w_in_t.shape[1]
    gq, gk, gv, gg, ga, rr, rk, rv, rw, ra, rg = jnp.split(
        w_in_t, np.cumsum([512, 512, 1024, 1024, 16, 1024, 1024, 1024, 96, 96]).tolist(), axis=0)
    mr, mk, mv, mw, ma, mg = jnp.split(mu_rwkv[None, :], np.cumsum([1024, 1024, 1024, 96, 96]).tolist(), axis=1)
    if vres_w_down_t is None:
        vres_w_down_t = jnp.zeros((RWKV_V_RANK, d), w_in_t.dtype)
        vres_mu = jnp.zeros((RWKV_V_RANK,), F32)
    w_pack_t = jnp.concatenate(
        [gq, gk, gv, gg, _pad_rows(ga, LANES), _pad_rows(vres_w_down_t, LANES),
         rr, rk, rv, rg, _pad_rows(rw, LANES), _pad_rows(ra, LANES)], axis=0)
    mu_pack = jnp.concatenate(
        [jnp.zeros((1, C_VRES), F32), _pad_cols(vres_mu[None, :], LANES),
         mr, mk, mv, mg, _pad_cols(mw, LANES), _pad_cols(ma, LANES)], axis=1)
    return w_pack_t, mu_pack


def kernel(x, c, w_ada, b_ada, g_pre_mix, g_post_mix, g_pre_ffn, g_post_ffn, w_in, gla_w_a_up, gla_b_a, gla_norm_w, rwkv_mu, rwkv_w0, rwkv_w_up, rwkv_a0, rwkv_a_up, rwkv_g_up, rwkv_k_k, rwkv_k_a, rwkv_r_k, rwkv_gn_w, rwkv_gn_b, vres_w_down, vres_mu, vres_up, vres_v0, w_out, w_ff1, w_ff2):
    bsz, t, d = x.shape
    assert bsz == 1 and d == D_MODEL and t % (CHUNK * max(GLA_NCH, RWKV_NCH)) == 0
    assert t % FFN_TM == 0 and t % INP_TM == 0 and t % OUT_TM == 0
    n_layers = w_ada.shape[0]

    mod = _ada_call(c.reshape(d, 1), w_ada, b_ada.reshape(n_layers, 1, 6 * d))

    p_gla = jnp.asarray(_gla_exponent_matrix(), BF16)
    gla_masks = tuple(jnp.asarray(m, BF16) for m in _gla_masks())
    tri = jnp.asarray(np.tril(np.ones((CHUNK, CHUNK), np.float32)), BF16)
    head_of_lane = np.arange(RWKV_GW) // RWKV_HEAD
    ones_bd = jnp.asarray((head_of_lane[:, None] == head_of_lane[None, :]).astype(np.float32), BF16)

    w_in_t = jnp.swapaxes(w_in, 1, 2).astype(BF16)
    w_out_b, w_ff1_b, w_ff2_b = w_out.astype(BF16), w_ff1.astype(BF16), w_ff2.astype(BF16)
    x2 = x.reshape(t, d)
    z_first = None
    for i in range(n_layers):
        j = i - 1
        w_pack_t, mu_pack = _pack_inproj(
            w_in_t[i], rwkv_mu[i],
            vres_w_down[j].T.astype(BF16) if i > 0 else None, vres_mu[j] if i > 0 else None)
        z = _inproj_call(x2, g_pre_mix[i][None, :], mod, i, w_pack_t, mu_pack)
        if i == 0:
            z_first = z

        o_gla = _gla_call(
            z, _pad_rows(gla_w_a_up[i], LANES).astype(BF16), gla_b_a[i][None, :], gla_norm_w[i][None, :], p_gla,
            gla_masks)

        row = lambda a: a.reshape(1, RWKV_WIDTH)
        vec = dict(w0=row(rwkv_w0[i]), a0=row(rwkv_a0[i]), k_k=row(rwkv_k_k[i]), k_a=row(rwkv_k_a[i]),
                   r_k=row(rwkv_r_k[i]), gn_w=row(rwkv_gn_w[i]), gn_b=row(rwkv_gn_b[i]))
        mats = dict(w_up=_pad_rows(rwkv_w_up[i], LANES).astype(BF16),
                    a_up=_pad_rows(rwkv_a_up[i], LANES).astype(BF16),
                    g_up=rwkv_g_up[i].astype(BF16))
        if i > 0:
            vec["v0"] = row(vres_v0[j])
            mats["v_up"] = _pad_rows(vres_up[j], LANES).astype(BF16)
        o_rwkv = _rwkv_call(z, z_first, vec, mats, tri, ones_bd, has_vres=i > 0)

        x2, h2 = _outproj_call(x2, o_gla, o_rwkv, w_out_b, g_post_mix[i][None, :], g_pre_ffn[i][None, :], mod, i)
        x2 = _ffn_call(x2, h2, g_post_ffn[i][None, :], mod, i, w_ff1_b, w_ff2_b)
    return x2.reshape(bsz, t, d)
```

```python
import functools

import numpy as np
import jax
import jax.numpy as jnp
from jax import lax
from jax.experimental import pallas as pl
from jax.experimental.pallas import tpu as pltpu

F32 = jnp.float32
BF16 = jnp.bfloat16

D_MODEL = 2048
DEPTH = 2
GLA_V = 1024
GLA_DV = 128
GLA_HEADS = 8
GLA_DK = 64
GLA_QK = 512
GLA_GATE_RANK = 16
GLA_TAU = 16.0
RWKV_WIDTH = 1024
RWKV_HEAD = 64
RWKV_HEADS = 16
RWKV_W_RANK = 96
RWKV_A_RANK = 96
RWKV_G_RANK = 256
RWKV_V_RANK = 64
RWKV_GN_EPS = 64e-5
D_FF = 4 * D_MODEL
RMS_EPS = 1e-6

LANES = 128
CHUNK = 64
VMEM_LIMIT_BYTES = 56 * 1024 * 1024

C_GQ = 0
C_GK = 512
C_GV = 1024
C_GG = 2048
C_RR = 3072
C_RK = 4096
C_RV = 5120
C_GA = 6144
C_VRES = 6272
C_RG = 6400
C_RW = 6656
C_RA = 6784
N_PACK = 6912
N_MAIN = 3072
N_EXTRA = N_PACK - 2 * N_MAIN


def _dot(a, b):
    return jnp.dot(a, b, preferred_element_type=F32)


def _dot_nt(a, b):
    return lax.dot_general(a, b, (((1,), (1,)), ((), ())), preferred_element_type=F32)


def _dot_split2_lhs(p, x):
    hi = x.astype(BF16)
    lo = (x - hi.astype(F32)).astype(BF16)
    y = _dot(p, jnp.concatenate([hi, lo], axis=1))
    return y[:, :x.shape[1]] + y[:, x.shape[1]:]


def _dot_split2_rhs(x, p):
    hi = x.astype(BF16)
    lo = (x - hi.astype(F32)).astype(BF16)
    y = _dot(jnp.concatenate([hi, lo], axis=0), p)
    return y[:x.shape[0]] + y[x.shape[0]:]


def _log_sigmoid(x):
    return jnp.minimum(x, 0.0) - jnp.log(1.0 + jnp.exp(-jnp.abs(x)))


def _sigmoid(x):
    return 1.0 / (1.0 + jnp.exp(-x))


ADA_TN = 1024


def _ada_kernel(c_ref, w_ref, b_ref, o_ref):
    c = c_ref[...]
    cond = c * _sigmoid(c)
    o_ref[0] = jnp.sum(w_ref[0] * cond, axis=0, keepdims=True) + b_ref[0]


def _ada_call(c_col, w_ada, b_ada3):
    n_layers, d, n = w_ada.shape
    return pl.pallas_call(
        _ada_kernel,
        out_shape=jax.ShapeDtypeStruct((n_layers, 1, n), F32),
        grid=(n_layers, n // ADA_TN),
        in_specs=[
            pl.BlockSpec((d, 1), lambda l, j: (0, 0)),
            pl.BlockSpec((1, d, ADA_TN), lambda l, j: (l, 0, j)),
            pl.BlockSpec((1, 1, ADA_TN), lambda l, j: (l, 0, j)),
        ],
        out_specs=pl.BlockSpec((1, 1, ADA_TN), lambda l, j: (l, 0, j)),
        compiler_params=pltpu.CompilerParams(
            dimension_semantics=("arbitrary", "arbitrary"), vmem_limit_bytes=VMEM_LIMIT_BYTES),
        name="ada_mod",
    )(c_col, w_ada, b_ada3)


INP_TM = 1024
INP_TN = 768


def _inproj_kernel(x_ref, g_ref, sh_ref, sc_ref, wa_ref, wb_ref, wc_ref, mu_ref, z_ref, h_scr, carry_scr):
    i = pl.program_id(0)
    j = pl.program_id(1)

    @pl.when(j == 0)
    def _():
        x = x_ref[...]
        ms = jnp.mean(x * x, axis=-1, keepdims=True)
        y = x * lax.rsqrt(ms + RMS_EPS) * g_ref[...]
        h_scr[...] = (y * (1.0 + sc_ref[0]) + sh_ref[0]).astype(BF16)

    @pl.when(i == 0)
    def _():
        carry_scr[j] = jnp.zeros(carry_scr.shape[1:], F32)

    n_main = N_MAIN // INP_TN
    w_tile = jnp.where(j < n_main, wa_ref[...], jnp.where(j < 2 * n_main, wb_ref[...], wc_ref[...]))
    z = _dot_nt(h_scr[...], w_tile)
    prev = carry_scr[j]
    row = lax.broadcasted_iota(jnp.int32, z.shape, 0)
    shifted = jnp.where(row == 0, prev, pltpu.roll(z, 1, 0))
    carry_scr[j] = z[z.shape[0] - 1:, :]
    z_ref[...] = (z + mu_ref[...] * (shifted - z)).astype(z_ref.dtype)


def _inproj_call(x2, g_row, mod, layer, w_in_t, w_run2_t, w_extra_t, mu_pack):
    t, d = x2.shape
    n = N_PACK
    nj = n // INP_TN
    n_main = N_MAIN // INP_TN
    assert N_MAIN % INP_TN == 0 and N_EXTRA == INP_TN
    return pl.pallas_call(
        _inproj_kernel,
        out_shape=jax.ShapeDtypeStruct((t, n), BF16),
        grid=(t // INP_TM, nj),
        in_specs=[
            pl.BlockSpec((INP_TM, d), lambda i, j: (i, 0)),
            pl.BlockSpec((1, d), lambda i, j: (0, 0)),
            pl.BlockSpec((1, 1, d), lambda i, j: (layer, 0, 0)),
            pl.BlockSpec((1, 1, d), lambda i, j: (layer, 0, 1)),
            pl.BlockSpec((None, INP_TN, d), lambda i, j: (layer, jnp.minimum(j, n_main - 1), 0)),
            pl.BlockSpec((None, INP_TN, d), lambda i, j: (layer, jnp.clip(j - n_main, 0, n_main - 1), 0)),
            pl.BlockSpec((None, INP_TN, d), lambda i, j: (layer, 0, 0)),
            pl.BlockSpec((1, INP_TN), lambda i, j: (0, j)),
        ],
        out_specs=pl.BlockSpec((INP_TM, INP_TN), lambda i, j: (i, j)),
        scratch_shapes=[pltpu.VMEM((INP_TM, d), BF16), pltpu.VMEM((nj, 1, INP_TN), F32)],
        compiler_params=pltpu.CompilerParams(
            dimension_semantics=("arbitrary", "arbitrary"), vmem_limit_bytes=VMEM_LIMIT_BYTES),
        name="inproj",
    )(x2, g_row, mod, mod, w_in_t, w_run2_t, w_extra_t, mu_pack)


GLA_NCH = 16
GLA_GROUP = 4
GLA_QW = GLA_GROUP * GLA_DK
GLA_VW = GLA_GROUP * GLA_DV
GLA_LEVELS = (32, 16, 8, 4, 2, 1)


def _gla_exponent_matrix():
    c = CHUNK
    p = np.zeros((2 * c + len(GLA_LEVELS) * c, c), np.float32)
    for i in range(c):
        p[i, : i + 1] = 1.0
        p[c + i, i + 1:] = 1.0
    for li, s in enumerate(GLA_LEVELS):
        base = 2 * c + li * c
        for i in range(c):
            m = (i // (2 * s)) * (2 * s) + s
            if i & s:
                p[base + i, m + 1: i + 1] = 1.0
            else:
                p[base + i, i + 1: m + 1] = 1.0
    return p


def _gla_masks():
    hq = np.arange(GLA_QW) // GLA_DK
    hv = np.arange(GLA_VW) // GLA_DV
    qq = (hq[:, None] == hq[None, :]).astype(np.float32)
    qv = (hq[:, None] == hv[None, :]).astype(np.float32)
    return qq, qv, qv.T.copy()


def _gla_kernel(q_ref, k_ref, v_ref, g_ref, al_ref, wup_ref, ba_ref, nw_ref, p_ref, mqq_ref, mqv_ref, mvq_ref,
                o_ref, st_scr):
    @pl.when(pl.program_id(1) == 0)
    def _():
        st_scr[...] = jnp.zeros(st_scr.shape, F32)

    c = CHUNK
    qw, vw = GLA_QW, GLA_VW
    tt = lax.broadcasted_iota(jnp.int32, (c, qw), 0)
    jj = lax.broadcasted_iota(jnp.int32, (c, qw), 1) & (GLA_DK - 1)
    mqq, mqv, mvq = mqq_ref[...], mqv_ref[...], mvq_ref[...]
    same_head_vq = ((lax.broadcasted_iota(jnp.int32, (vw, qw), 0) >> int(np.log2(GLA_DV)))
                    == (lax.broadcasted_iota(jnp.int32, (vw, qw), 1) >> int(np.log2(GLA_DK))))

    def bd_qk(x):
        return jnp.concatenate([x.astype(BF16)] * GLA_GROUP, axis=0) * mqq

    x = _dot(al_ref[...].astype(BF16), wup_ref[...]) + ba_ref[...]
    la_all = _log_sigmoid(x) * (1.0 / GLA_TAU)
    q_all = q_ref[...].astype(F32) * (GLA_DK ** -0.5)
    k_all = k_ref[...].astype(F32)
    v_all = v_ref[...].astype(F32)

    chunks = range(GLA_NCH)
    rows = [slice(ci * c, (ci + 1) * c) for ci in chunks]
    e_all = [_dot_split2_lhs(p_ref[...], la_all[rw]) for rw in rows]
    q = [q_all[rw] for rw in rows]
    k = [k_all[rw] for rw in rows]

    qk_sum = _dot_split2_rhs(q_all * k_all, mqq)
    scores = [jnp.where(tt == jj, qk_sum[rw], 0.0) for rw in rows]
    for li, s in enumerate(GLA_LEVELS):
        second = (tt & s) != 0
        shift = int(np.log2(2 * s))
        same = (tt >> shift) == (jj >> shift)
        e = [jnp.exp(e_all[ci][(2 + li) * c:(3 + li) * c]) for ci in chunks]
        qd = [jnp.where(second, q[ci] * e[ci], 0.0).astype(BF16) for ci in chunks]
        kd = [bd_qk(jnp.where(second, 0.0, k[ci] * e[ci])) for ci in chunks]
        scores = [scores[ci] + jnp.where(same, _dot_nt(qd[ci], kd[ci]), 0.0) for ci in chunks]

    v_bd = [jnp.concatenate([v_all[rw].astype(BF16)] * GLA_GROUP, axis=0) * mqv for rw in rows]
    o_intra = [_dot(scores[ci].astype(BF16), v_bd[ci]) for ci in chunks]
    upd = []
    for ci in chunks:
        ke = (k[ci] * jnp.exp(e_all[ci][c:2 * c])).astype(BF16)
        m = jnp.where(same_head_vq, _dot(v_all[rows[ci]].T.astype(BF16), ke), 0.0)
        dv = GLA_DV
        upd.append(m[0:dv] + m[dv:2 * dv] + m[2 * dv:3 * dv] + m[3 * dv:4 * dv])

    st = st_scr[...]
    os_ = []
    for ci in chunks:
        b = e_all[ci][0:c]
        st_bd = jnp.concatenate([st.astype(BF16)] * GLA_GROUP, axis=0) * mvq
        os_.append(_dot_nt((q[ci] * jnp.exp(b)).astype(BF16), st_bd) + o_intra[ci])
        st = st * jnp.exp(b[c - 1:c, :]) + upd[ci]
    st_scr[...] = st

    o = jnp.concatenate(os_, axis=0)
    g = g_ref[...].astype(F32)
    gs = g * _sigmoid(g) * nw_ref[...]
    outs = []
    for h in range(GLA_GROUP):
        oh = o[:, h * GLA_DV:(h + 1) * GLA_DV]
        oh = oh * lax.rsqrt(jnp.mean(oh * oh, axis=-1, keepdims=True) + RMS_EPS)
        outs.append(oh * gs[:, h * GLA_DV:(h + 1) * GLA_DV])
    o_ref[...] = jnp.concatenate(outs, axis=1).astype(o_ref.dtype)


def _gla_call(z, wup_pad, ba_row, nw_row, p_mat, masks):
    t = z.shape[0]
    tb = CHUNK * GLA_NCH
    groups = GLA_HEADS // GLA_GROUP
    blk = lambda w, off: pl.BlockSpec((tb, w), lambda p, c: (c, off // w + p))
    const = lambda a: pl.BlockSpec(a.shape, lambda p, c: (0, 0))
    return pl.pallas_call(
        _gla_kernel,
        out_shape=jax.ShapeDtypeStruct((t, GLA_V), BF16),
        grid=(groups, t // tb),
        in_specs=[
            blk(GLA_QW, C_GQ), blk(GLA_QW, C_GK), blk(GLA_VW, C_GV), blk(GLA_VW, C_GG),
            pl.BlockSpec((tb, LANES), lambda p, c: (c, C_GA // LANES)),
            pl.BlockSpec((LANES, GLA_QW), lambda p, c: (0, p)),
            pl.BlockSpec((1, GLA_QW), lambda p, c: (0, p)),
            pl.BlockSpec((1, GLA_VW), lambda p, c: (0, p)),
            const(p_mat), const(masks[0]), const(masks[1]), const(masks[2]),
        ],
        out_specs=pl.BlockSpec((tb, GLA_VW), lambda p, c: (c, p)),
        scratch_shapes=[pltpu.VMEM((GLA_DV, GLA_QW), F32)],
        compiler_params=pltpu.CompilerParams(
            dimension_semantics=("arbitrary", "arbitrary"), vmem_limit_bytes=VMEM_LIMIT_BYTES),
        name="gla_mixer",
    )(z, z, z, z, z, wup_pad, ba_row, nw_row, p_mat, *masks)


RWKV_NCH = 16
RWKV_GROUP = 4
RWKV_GW = RWKV_GROUP * RWKV_HEAD
RWKV_INV_LEVELS = (2, 4, 8, 16, 32)


def _rwkv_kernel(*refs, has_vres):
    if has_vres:
        (r_ref, k_ref, v_ref, wl_ref, al_ref, gl_ref, vl_ref, vf_ref, vup_ref, v0_ref,
         w0_ref, wup_ref, a0_ref, aup_ref, gup_ref, kk_ref, ka_ref, rk_ref, gnw_ref, gnb_ref,
         tri_ref, ones_ref, o_ref, st_scr) = refs
    else:
        (r_ref, k_ref, v_ref, wl_ref, al_ref, gl_ref,
         w0_ref, wup_ref, a0_ref, aup_ref, gup_ref, kk_ref, ka_ref, rk_ref, gnw_ref, gnb_ref,
         tri_ref, ones_ref, o_ref, st_scr) = refs

    @pl.when(pl.program_id(1) == 0)
    def _():
        st_scr[...] = jnp.zeros(st_scr.shape, F32)

    c = CHUNK
    gw = RWKV_GW
    tt = lax.broadcasted_iota(jnp.int32, (c, gw), 0)
    jj = lax.broadcasted_iota(jnp.int32, (c, gw), 1) & (RWKV_HEAD - 1)
    strict = tt > jj
    incl = tt >= jj
    ones_bd = ones_ref[...]
    same_head_b = ones_bd != 0
    hshift = int(np.log2(RWKV_HEAD))
    same_head = ((lax.broadcasted_iota(jnp.int32, (gw, gw), 0) >> hshift)
                 == (lax.broadcasted_iota(jnp.int32, (gw, gw), 1) >> hshift))
    inv_n = 1.0 / RWKV_HEAD

    def seg_sum(x):
        return _dot_split2_rhs(x, ones_bd)

    def block_diag(x):
        return jnp.concatenate([x.astype(BF16)] * RWKV_GROUP, axis=0) * ones_bd

    tb = c * RWKV_NCH
    r_all = r_ref[...].astype(F32)
    k_all = k_ref[...].astype(F32)
    v_all = v_ref[...].astype(F32)
    w_pre = w0_ref[...] + _dot(jnp.tanh(wl_ref[...].astype(F32)).astype(BF16), wup_ref[...])
    lw_all = -jnp.exp(_log_sigmoid(w_pre) - 0.5)
    alr = _sigmoid(a0_ref[...] + _dot(al_ref[...].astype(BF16), aup_ref[...]))
    gate_all = _dot(_sigmoid(gl_ref[...].astype(F32)).astype(BF16), gup_ref[...])
    if has_vres:
        mix = _sigmoid(v0_ref[...] + _dot(vl_ref[...].astype(BF16), vup_ref[...]))
        v_all = v_all + (vf_ref[...].astype(F32) - v_all) * mix
    kk = k_all * kk_ref[...]
    k2_all = k_all * (1.0 + (alr - 1.0) * ka_ref[...])
    sums = seg_sum(jnp.concatenate([kk * kk, r_all * k2_all * rk_ref[...]], axis=0))
    kk = kk * lax.rsqrt(jnp.maximum(sums[:tb], 1e-24))
    bonus_all = sums[tb:] * v_all
    a_all = -kk
    b_all = kk * alr

    chunks = range(RWKV_NCH)
    rows = [slice(ci * c, (ci + 1) * c) for ci in chunks]
    cl = [_dot_split2_lhs(tri_ref[...], lw_all[rw]) for rw in rows]
    pre = []
    for ci in chunks:
        rw = rows[ci]
        cl_last = cl[ci][c - 1:c, :]
        e_neg = jnp.exp(-cl[ci])
        e_end = jnp.exp(cl_last - cl[ci])
        b, k2, v = b_all[rw], k2_all[rw], v_all[rw]
        at = a_all[rw] * jnp.exp(cl[ci] - lw_all[rw])
        rt = r_all[rw] * jnp.exp(cl[ci])
        pre.append(dict(
            rt=rt, v=v, at=at, g_c=jnp.exp(cl_last),
            lhs=jnp.concatenate([at, rt], axis=0).astype(BF16),
            rhs=jnp.concatenate([block_diag(b * e_neg), block_diag(k2 * e_neg)], axis=0),
            hat=jnp.concatenate([b * e_end, k2 * e_end], axis=0).astype(BF16)))
    for p in pre:
        aa = _dot_nt(p["lhs"], p["rhs"])
        p.update(
            a_ab=jnp.where(strict, aa[:c, :gw], 0.0),
            a_ak=jnp.where(strict, aa[:c, gw:], 0.0).astype(BF16),
            p_rb=jnp.where(incl, aa[c:, :gw], 0.0).astype(BF16),
            p_rk=jnp.where(incl, aa[c:, gw:], 0.0).astype(BF16))

    a_bd = [block_diag(p["a_ab"]) for p in pre]
    tinv = [jnp.where(tt == jj, 1.0, jnp.where((tt >> 1) == (jj >> 1), p["a_ab"], 0.0)) for p in pre]
    for s in RWKV_INV_LEVELS:
        shift = int(np.log2(2 * s))
        off = ((tt >> shift) == (jj >> shift)) & ((tt & s) != 0) & ((jj & s) == 0)
        t_bd = [block_diag(t) for t in tinv]
        half = [jnp.where(off, _dot(tinv[ci].astype(BF16), a_bd[ci]), 0.0).astype(BF16) for ci in chunks]
        tinv = [tinv[ci] + _dot(half[ci], t_bd[ci]) for ci in chunks]

    t_b = [t.astype(BF16) for t in tinv]
    v_bd = [block_diag(p["v"]) for p in pre]
    wt = [_dot(t_b[ci], block_diag(pre[ci]["at"])) for ci in chunks]
    akv = [block_diag(_dot(pre[ci]["a_ak"], v_bd[ci])) for ci in chunks]
    u0 = [_dot(t_b[ci], akv[ci]) for ci in chunks]
    qh = [(pre[ci]["rt"] + _dot(pre[ci]["p_rb"], block_diag(wt[ci]))).astype(BF16) for ci in chunks]
    gmat = [jnp.where(same_head, _dot(wt[ci].T.astype(BF16), pre[ci]["hat"][:c]), 0.0).astype(BF16)
            for ci in chunks]
    y0 = [_dot(pre[ci]["p_rb"], block_diag(u0[ci])) + _dot(pre[ci]["p_rk"], v_bd[ci]) for ci in chunks]
    n0c = []
    for ci in chunks:
        uv_t = jnp.concatenate([u0[ci], pre[ci]["v"]], axis=0).T.astype(BF16)
        n0 = jnp.where(same_head, _dot(uv_t, pre[ci]["hat"]), 0.0)
        n0c.append(n0[0:c] + n0[c:2 * c] + n0[2 * c:3 * c] + n0[3 * c:4 * c])

    st = st_scr[...]
    ys = []
    for ci in chunks:
        ys.append(_dot_nt(qh[ci], block_diag(st)) + y0[ci])
        st = st * pre[ci]["g_c"] + _dot(st.astype(BF16), gmat[ci]) + n0c[ci]
    st_scr[...] = st

    y = jnp.concatenate(ys, axis=0)
    yc = y - seg_sum(y) * inv_n
    var = seg_sum(yc * yc) * inv_n
    yn = yc * lax.rsqrt(var + RWKV_GN_EPS) * gnw_ref[...] + gnb_ref[...]
    o_ref[...] = ((yn + bonus_all) * gate_all).astype(o_ref.dtype)


def _rwkv_call(z, z_first, vec, mats, tri, ones_bd, has_vres):
    t = z.shape[0]
    tb = CHUNK * RWKV_NCH
    gw = RWKV_GW
    groups = RWKV_HEADS // RWKV_GROUP
    col = lambda off: pl.BlockSpec((tb, gw), lambda p, c: (c, off // gw + p))
    fixed = lambda off, w: pl.BlockSpec((tb, w), lambda p, c: (c, off // w))
    rowv = pl.BlockSpec((1, gw), lambda p, c: (0, p))
    upm = lambda rows: pl.BlockSpec((rows, gw), lambda p, c: (0, p))
    const = lambda a: pl.BlockSpec(a.shape, lambda p, c: (0, 0))

    in_specs = [col(C_RR), col(C_RK), col(C_RV), fixed(C_RW, LANES), fixed(C_RA, LANES), fixed(C_RG, 2 * LANES)]
    args = [z, z, z, z, z, z]
    if has_vres:
        in_specs += [fixed(C_VRES, LANES), col(C_RV), upm(LANES), rowv]
        args += [z, z_first, mats["v_up"], vec["v0"]]
    in_specs += [rowv, upm(LANES), rowv, upm(LANES), upm(2 * LANES), rowv, rowv, rowv, rowv, rowv,
                 const(tri), const(ones_bd)]
    args += [vec["w0"], mats["w_up"], vec["a0"], mats["a_up"], mats["g_up"], vec["k_k"], vec["k_a"],
             vec["r_k"], vec["gn_w"], vec["gn_b"], tri, ones_bd]
    return pl.pallas_call(
        functools.partial(_rwkv_kernel, has_vres=has_vres),
        out_shape=jax.ShapeDtypeStruct((t, RWKV_WIDTH), BF16),
        grid=(groups, t // tb),
        in_specs=in_specs,
        out_specs=pl.BlockSpec((tb, gw), lambda p, c: (c, p)),
        scratch_shapes=[pltpu.VMEM((CHUNK, RWKV_GW), F32)],
        compiler_params=pltpu.CompilerParams(
            dimension_semantics=("arbitrary", "arbitrary"), vmem_limit_bytes=VMEM_LIMIT_BYTES),
        name="rwkv7_mixer",
    )(*args)


OUT_TM = 512
OUT_SUB = 256


def _outproj_kernel(x_ref, oa_ref, ob_ref, wa_ref, wb_ref, g_ref, gt_ref, gpre_ref, sh_ref, sc_ref, o_ref, h_ref):
    subs = [slice(s, s + OUT_SUB) for s in range(0, OUT_TM, OUT_SUB)]
    ys = [_dot(oa_ref[rw, :], wa_ref[...]) + _dot(ob_ref[rw, :], wb_ref[...]) for rw in subs]
    post_scale = gt_ref[0] * g_ref[...]
    pre_scale = gpre_ref[...] * (1.0 + sc_ref[0])
    for rw, y in zip(subs, ys):
        ms = jnp.mean(y * y, axis=-1, keepdims=True)
        xn = x_ref[rw, :] + (y * lax.rsqrt(ms + RMS_EPS)) * post_scale
        o_ref[rw, :] = xn
        ms2 = jnp.mean(xn * xn, axis=-1, keepdims=True)
        h_ref[rw, :] = ((xn * lax.rsqrt(ms2 + RMS_EPS)) * pre_scale + sh_ref[0]).astype(h_ref.dtype)


def _outproj_call(x2, o_gla, o_rwkv, w_out_all, g_row, gpre_row, mod, layer):
    t, d = x2.shape
    assert GLA_V == RWKV_WIDTH
    return pl.pallas_call(
        _outproj_kernel,
        out_shape=(jax.ShapeDtypeStruct((t, d), F32), jax.ShapeDtypeStruct((t, d), BF16)),
        grid=(t // OUT_TM,),
        in_specs=[
            pl.BlockSpec((OUT_TM, d), lambda i: (i, 0)),
            pl.BlockSpec((OUT_TM, GLA_V), lambda i: (i, 0)),
            pl.BlockSpec((OUT_TM, RWKV_WIDTH), lambda i: (i, 0)),
            pl.BlockSpec((None, GLA_V, d), lambda i: (layer, 0, 0)),
            pl.BlockSpec((None, RWKV_WIDTH, d), lambda i: (layer, 1, 0)),
            pl.BlockSpec((1, d), lambda i: (0, 0)),
            pl.BlockSpec((1, 1, d), lambda i: (layer, 0, 2)),
            pl.BlockSpec((1, d), lambda i: (0, 0)),
            pl.BlockSpec((1, 1, d), lambda i: (layer, 0, 3)),
            pl.BlockSpec((1, 1, d), lambda i: (layer, 0, 4)),
        ],
        out_specs=(pl.BlockSpec((OUT_TM, d), lambda i: (i, 0)), pl.BlockSpec((OUT_TM, d), lambda i: (i, 0))),
        compiler_params=pltpu.CompilerParams(
            dimension_semantics=("arbitrary",), vmem_limit_bytes=VMEM_LIMIT_BYTES),
        name="outproj",
    )(x2, o_gla, o_rwkv, w_out_all, w_out_all, g_row, mod, gpre_row, mod, mod)


FFN_TM = 512
FFN_TF = 1024


def _ffn_kernel(x_ref, h_ref, w1_ref, w2_ref, gpost_ref, gt_ref, o_ref, acc_scr):
    f = pl.program_id(1)

    @pl.when(f == 0)
    def _():
        acc_scr[...] = jnp.zeros(acc_scr.shape, F32)

    u = jnp.maximum(_dot(h_ref[...], w1_ref[...]), 0.0)
    acc_scr[...] += _dot((u * u).astype(BF16), w2_ref[...])

    @pl.when(f == pl.num_programs(1) - 1)
    def _():
        y = acc_scr[...]
        ms = jnp.mean(y * y, axis=-1, keepdims=True)
        o_ref[...] = x_ref[...] + gt_ref[0] * (y * lax.rsqrt(ms + RMS_EPS) * gpost_ref[...])


def _ffn_call(x2, h2, gpost_row, mod, layer, w1, w2):
    t, d = x2.shape
    dff = w1.shape[2]
    return pl.pallas_call(
        _ffn_kernel,
        out_shape=jax.ShapeDtypeStruct((t, d), F32),
        grid=(t // FFN_TM, dff // FFN_TF),
        in_specs=[
            pl.BlockSpec((FFN_TM, d), lambda i, f: (i, 0)),
            pl.BlockSpec((FFN_TM, d), lambda i, f: (i, 0)),
            pl.BlockSpec((None, d, FFN_TF), lambda i, f: (layer, 0, f)),
            pl.BlockSpec((None, FFN_TF, d), lambda i, f: (layer, f, 0)),
            pl.BlockSpec((1, d), lambda i, f: (0, 0)),
            pl.BlockSpec((1, 1, d), lambda i, f: (layer, 0, 5)),
        ],
        out_specs=pl.BlockSpec((FFN_TM, d), lambda i, f: (i, 0)),
        scratch_shapes=[pltpu.VMEM((FFN_TM, d), F32)],
        compiler_params=pltpu.CompilerParams(
            dimension_semantics=("arbitrary", "arbitrary"), vmem_limit_bytes=VMEM_LIMIT_BYTES),
        name="ffn",
    )(x2, h2, w1, w2, gpost_row, mod)


def _pad_cols(w, width):
    return jnp.pad(w, ((0, 0), (0, width - w.shape[1])))


def _pad_rows(w, rows):
    return jnp.pad(w, ((0, rows - w.shape[0]), (0, 0)))


def _pack_inproj(w_in_t, rwkv_mu, vres_w_down, vres_mu):
    n_layers, _, d = w_in_t.shape
    gla_cols = N_MAIN + GLA_GATE_RANK
    rkv_end = gla_cols + N_MAIN
    w_end, a_end = rkv_end + RWKV_W_RANK, rkv_end + RWKV_W_RANK + RWKV_A_RANK
    pad_to = lambda w, n: jnp.pad(w, ((0, 0), (0, n - w.shape[1]), (0, 0)))
    vres_t = jnp.concatenate(
        [jnp.zeros((1, RWKV_V_RANK, d), BF16), jnp.swapaxes(vres_w_down, 1, 2).astype(BF16)], axis=0)
    w_run2_t = w_in_t[:, gla_cols:rkv_end]
    w_extra_t = jnp.concatenate(
        [pad_to(w_in_t[:, N_MAIN:gla_cols], LANES), pad_to(vres_t, LANES), w_in_t[:, a_end:],
         pad_to(w_in_t[:, rkv_end:w_end], LANES), pad_to(w_in_t[:, w_end:a_end], LANES)], axis=1)
    vmu = jnp.concatenate([jnp.zeros((1, RWKV_V_RANK), F32), vres_mu], axis=0)
    padc = lambda m, n: jnp.pad(m, ((0, 0), (0, n - m.shape[1])))
    mu_pack = jnp.concatenate(
        [jnp.zeros((n_layers, N_MAIN), F32), rwkv_mu[:, :N_MAIN], jnp.zeros((n_layers, LANES), F32), padc(vmu, LANES),
         rwkv_mu[:, N_MAIN + RWKV_W_RANK + RWKV_A_RANK:], padc(rwkv_mu[:, N_MAIN:N_MAIN + RWKV_W_RANK], LANES),
         padc(rwkv_mu[:, N_MAIN + RWKV_W_RANK:N_MAIN + RWKV_W_RANK + RWKV_A_RANK], LANES)], axis=1)
    return w_run2_t, w_extra_t, mu_pack[:, None, :]


def kernel(x, c, w_ada, b_ada, g_pre_mix, g_post_mix, g_pre_ffn, g_post_ffn, w_in, gla_w_a_up, gla_b_a, gla_norm_w, rwkv_mu, rwkv_w0, rwkv_w_up, rwkv_a0, rwkv_a_up, rwkv_g_up, rwkv_k_k, rwkv_k_a, rwkv_r_k, rwkv_gn_w, rwkv_gn_b, vres_w_down, vres_mu, vres_up, vres_v0, w_out, w_ff1, w_ff2):
    bsz, t, d = x.shape
    assert bsz == 1 and d == D_MODEL and t % (CHUNK * max(GLA_NCH, RWKV_NCH)) == 0
    assert t % FFN_TM == 0 and t % INP_TM == 0 and t % OUT_TM == 0
    n_layers = w_ada.shape[0]

    mod = _ada_call(c.reshape(d, 1), w_ada, b_ada.reshape(n_layers, 1, 6 * d))

    p_gla = jnp.asarray(_gla_exponent_matrix(), BF16)
    gla_masks = tuple(jnp.asarray(m, BF16) for m in _gla_masks())
    tri = jnp.asarray(np.tril(np.ones((CHUNK, CHUNK), np.float32)), BF16)
    head_of_lane = np.arange(RWKV_GW) // RWKV_HEAD
    ones_bd = jnp.asarray((head_of_lane[:, None] == head_of_lane[None, :]).astype(np.float32), BF16)

    w_in_t = jnp.swapaxes(w_in, 1, 2).astype(BF16)
    w_run2_t, w_extra_t, mu_pack = _pack_inproj(w_in_t, rwkv_mu, vres_w_down, vres_mu)
    w_out_b, w_ff1_b, w_ff2_b = w_out.astype(BF16), w_ff1.astype(BF16), w_ff2.astype(BF16)
    x2 = x.reshape(t, d)
    z_first = None
    for i in range(n_layers):
        j = i - 1
        z = _inproj_call(x2, g_pre_mix[i][None, :], mod, i, w_in_t, w_run2_t, w_extra_t, mu_pack[i])
        if i == 0:
            z_first = z

        o_gla = _gla_call(
            z, _pad_rows(gla_w_a_up[i], LANES).astype(BF16), gla_b_a[i][None, :], gla_norm_w[i][None, :], p_gla,
            gla_masks)

        row = lambda a: a.reshape(1, RWKV_WIDTH)
        vec = dict(w0=row(rwkv_w0[i]), a0=row(rwkv_a0[i]), k_k=row(rwkv_k_k[i]), k_a=row(rwkv_k_a[i]),
                   r_k=row(rwkv_r_k[i]), gn_w=row(rwkv_gn_w[i]), gn_b=row(rwkv_gn_b[i]))
        mats = dict(w_up=_pad_rows(rwkv_w_up[i], LANES).astype(BF16),
                    a_up=_pad_rows(rwkv_a_up[i], LANES).astype(BF16),
                    g_up=rwkv_g_up[i].astype(BF16))
        if i > 0:
            vec["v0"] = row(vres_v0[j])
            mats["v_up"] = _pad_rows(vres_up[j], LANES).astype(BF16)
        o_rwkv = _rwkv_call(z, z_first, vec, mats, tri, ones_bd, has_vres=i > 0)

        x2, h2 = _outproj_call(x2, o_gla, o_rwkv, w_out_b, g_post_mix[i][None, :], g_pre_ffn[i][None, :], mod, i)
        x2 = _ffn_call(x2, h2, g_post_ffn[i][None, :], mod, i, w_ff1_b, w_ff2_b)
    return x2.reshape(bsz, t, d)
```

```python
import functools

import numpy as np
import jax
import jax.numpy as jnp
from jax import lax
from jax.experimental import pallas as pl
from jax.experimental.pallas import tpu as pltpu

F32 = jnp.float32
BF16 = jnp.bfloat16

D_MODEL = 2048
DEPTH = 2
GLA_V = 1024
GLA_DV = 128
GLA_HEADS = 8
GLA_DK = 64
GLA_QK = 512
GLA_GATE_RANK = 16
GLA_TAU = 16.0
RWKV_WIDTH = 1024
RWKV_HEAD = 64
RWKV_HEADS = 16
RWKV_W_RANK = 96
RWKV_A_RANK = 96
RWKV_G_RANK = 256
RWKV_V_RANK = 64
RWKV_GN_EPS = 64e-5
D_FF = 4 * D_MODEL
RMS_EPS = 1e-6

LANES = 128
CHUNK = 64
VMEM_LIMIT_BYTES = 56 * 1024 * 1024

C_GQ = 0
C_GK = 512
C_GV = 1024
C_GG = 2048
C_RR = 3072
C_RK = 4096
C_RV = 5120
C_GA = 6144
C_VRES = 6272
C_RG = 6400
C_RW = 6656
C_RA = 6784
N_PACK = 6912
N_MAIN = 3072
N_EXTRA = N_PACK - 2 * N_MAIN


def _dot(a, b):
    return jnp.dot(a, b, preferred_element_type=F32)


def _dot_nt(a, b):
    return lax.dot_general(a, b, (((1,), (1,)), ((), ())), preferred_element_type=F32)


def _dot_split2_lhs(p, x):
    hi = x.astype(BF16)
    lo = (x - hi.astype(F32)).astype(BF16)
    y = _dot(p, jnp.concatenate([hi, lo], axis=1))
    return y[:, :x.shape[1]] + y[:, x.shape[1]:]


def _dot_split2_rhs(x, p):
    hi = x.astype(BF16)
    lo = (x - hi.astype(F32)).astype(BF16)
    y = _dot(jnp.concatenate([hi, lo], axis=0), p)
    return y[:x.shape[0]] + y[x.shape[0]:]


def _log_sigmoid(x):
    return jnp.minimum(x, 0.0) - jnp.log(1.0 + jnp.exp(-jnp.abs(x)))


def _sigmoid(x):
    return 1.0 / (1.0 + jnp.exp(-x))


ADA_TN = 1024


def _ada_kernel(c_ref, w_ref, b_ref, o_ref):
    c = c_ref[...]
    cond = c * _sigmoid(c)
    o_ref[0] = jnp.sum(w_ref[0] * cond, axis=0, keepdims=True) + b_ref[0]


def _ada_call(c_col, w_ada, b_ada3):
    n_layers, d, n = w_ada.shape
    return pl.pallas_call(
        _ada_kernel,
        out_shape=jax.ShapeDtypeStruct((n_layers, 1, n), F32),
        grid=(n_layers, n // ADA_TN),
        in_specs=[
            pl.BlockSpec((d, 1), lambda l, j: (0, 0)),
            pl.BlockSpec((1, d, ADA_TN), lambda l, j: (l, 0, j)),
            pl.BlockSpec((1, 1, ADA_TN), lambda l, j: (l, 0, j)),
        ],
        out_specs=pl.BlockSpec((1, 1, ADA_TN), lambda l, j: (l, 0, j)),
        compiler_params=pltpu.CompilerParams(
            dimension_semantics=("arbitrary", "arbitrary"), vmem_limit_bytes=VMEM_LIMIT_BYTES),
        name="ada_mod",
    )(c_col, w_ada, b_ada3)


INP_TM = 1024
INP_TN = 768


def _inproj_kernel(x_ref, g_ref, sh_ref, sc_ref, wa_ref, wb_ref, wc_ref, mu_ref, z_ref, h_scr, carry_scr):
    i = pl.program_id(0)
    j = pl.program_id(1)

    @pl.when(j == 0)
    def _():
        x = x_ref[...]
        ms = jnp.mean(x * x, axis=-1, keepdims=True)
        y = x * lax.rsqrt(ms + RMS_EPS) * g_ref[...]
        h_scr[...] = (y * (1.0 + sc_ref[0]) + sh_ref[0]).astype(BF16)

    @pl.when(i == 0)
    def _():
        carry_scr[j] = jnp.zeros(carry_scr.shape[1:], F32)

    n_main = N_MAIN // INP_TN
    w_tile = jnp.where(j < n_main, wa_ref[...], jnp.where(j < 2 * n_main, wb_ref[...], wc_ref[...]))
    z = _dot_nt(h_scr[...], w_tile)
    prev = carry_scr[j]
    row = lax.broadcasted_iota(jnp.int32, z.shape, 0)
    shifted = jnp.where(row == 0, prev, pltpu.roll(z, 1, 0))
    carry_scr[j] = z[z.shape[0] - 1:, :]
    z_ref[...] = (z + mu_ref[...] * (shifted - z)).astype(z_ref.dtype)


def _inproj_call(x2, g_row, mod, layer, w_in_t, w_run2_t, w_extra_t, mu_pack):
    t, d = x2.shape
    n = N_PACK
    nj = n // INP_TN
    n_main = N_MAIN // INP_TN
    assert N_MAIN % INP_TN == 0 and N_EXTRA == INP_TN
    return pl.pallas_call(
        _inproj_kernel,
        out_shape=jax.ShapeDtypeStruct((t, n), BF16),
        grid=(t // INP_TM, nj),
        in_specs=[
            pl.BlockSpec((INP_TM, d), lambda i, j: (i, 0)),
            pl.BlockSpec((1, d), lambda i, j: (0, 0)),
            pl.BlockSpec((1, 1, d), lambda i, j: (layer, 0, 0)),
            pl.BlockSpec((1, 1, d), lambda i, j: (layer, 0, 1)),
            pl.BlockSpec((None, INP_TN, d), lambda i, j: (layer, jnp.minimum(j, n_main - 1), 0)),
            pl.BlockSpec((None, INP_TN, d), lambda i, j: (layer, jnp.clip(j - n_main, 0, n_main - 1), 0)),
            pl.BlockSpec((None, INP_TN, d), lambda i, j: (layer, 0, 0)),
            pl.BlockSpec((1, INP_TN), lambda i, j: (0, j)),
        ],
        out_specs=pl.BlockSpec((INP_TM, INP_TN), lambda i, j: (i, j)),
        scratch_shapes=[pltpu.VMEM((INP_TM, d), BF16), pltpu.VMEM((nj, 1, INP_TN), F32)],
        compiler_params=pltpu.CompilerParams(
            dimension_semantics=("arbitrary", "arbitrary"), vmem_limit_bytes=VMEM_LIMIT_BYTES),
        name="inproj",
    )(x2, g_row, mod, mod, w_in_t, w_run2_t, w_extra_t, mu_pack)


GLA_NCH = 16
GLA_GROUP = 4
GLA_QW = GLA_GROUP * GLA_DK
GLA_VW = GLA_GROUP * GLA_DV
GLA_LEVELS = (32, 16, 8, 4, 2, 1)


def _gla_exponent_matrix():
    c = CHUNK
    p = np.zeros((2 * c + len(GLA_LEVELS) * c, c), np.float32)
    for i in range(c):
        p[i, : i + 1] = 1.0
        p[c + i, i + 1:] = 1.0
    for li, s in enumerate(GLA_LEVELS):
        base = 2 * c + li * c
        for i in range(c):
            m = (i // (2 * s)) * (2 * s) + s
            if i & s:
                p[base + i, m + 1: i + 1] = 1.0
            else:
                p[base + i, i + 1: m + 1] = 1.0
    return p


def _gla_masks():
    hq = np.arange(GLA_QW) // GLA_DK
    hv = np.arange(GLA_VW) // GLA_DV
    qq = (hq[:, None] == hq[None, :]).astype(np.float32)
    qv = (hq[:, None] == hv[None, :]).astype(np.float32)
    return qq, qv, qv.T.copy()


def _gla_kernel(q_ref, k_ref, v_ref, g_ref, al_ref, wup_ref, ba_ref, nw_ref, p_ref, mqq_ref, mqv_ref, mvq_ref,
                w1_ref, w2_ref, o_ref, w1b_ref, w2b_ref, st_scr):
    @pl.when(pl.program_id(1) == 0)
    def _():
        st_scr[...] = jnp.zeros(st_scr.shape, F32)

    w1b_ref[...] = w1_ref[...].astype(BF16)
    w2b_ref[...] = w2_ref[...].astype(BF16)

    c = CHUNK
    qw, vw = GLA_QW, GLA_VW
    tt = lax.broadcasted_iota(jnp.int32, (c, qw), 0)
    jj = lax.broadcasted_iota(jnp.int32, (c, qw), 1) & (GLA_DK - 1)
    mqq, mqv, mvq = mqq_ref[...], mqv_ref[...], mvq_ref[...]
    same_head_vq = ((lax.broadcasted_iota(jnp.int32, (vw, qw), 0) >> int(np.log2(GLA_DV)))
                    == (lax.broadcasted_iota(jnp.int32, (vw, qw), 1) >> int(np.log2(GLA_DK))))

    def bd_qk(x):
        return jnp.concatenate([x.astype(BF16)] * GLA_GROUP, axis=0) * mqq

    x = _dot(al_ref[...].astype(BF16), wup_ref[...]) + ba_ref[...]
    la_all = _log_sigmoid(x) * (1.0 / GLA_TAU)
    q_all = q_ref[...].astype(F32) * (GLA_DK ** -0.5)
    k_all = k_ref[...].astype(F32)
    v_all = v_ref[...].astype(F32)

    chunks = range(GLA_NCH)
    rows = [slice(ci * c, (ci + 1) * c) for ci in chunks]
    e_all = [_dot_split2_lhs(p_ref[...], la_all[rw]) for rw in rows]
    q = [q_all[rw] for rw in rows]
    k = [k_all[rw] for rw in rows]

    qk_sum = _dot_split2_rhs(q_all * k_all, mqq)
    scores = [jnp.where(tt == jj, qk_sum[rw], 0.0) for rw in rows]
    for li, s in enumerate(GLA_LEVELS):
        second = (tt & s) != 0
        shift = int(np.log2(2 * s))
        same = (tt >> shift) == (jj >> shift)
        e = [jnp.exp(e_all[ci][(2 + li) * c:(3 + li) * c]) for ci in chunks]
        qd = [jnp.where(second, q[ci] * e[ci], 0.0).astype(BF16) for ci in chunks]
        kd = [bd_qk(jnp.where(second, 0.0, k[ci] * e[ci])) for ci in chunks]
        scores = [scores[ci] + jnp.where(same, _dot_nt(qd[ci], kd[ci]), 0.0) for ci in chunks]

    v_bd = [jnp.concatenate([v_all[rw].astype(BF16)] * GLA_GROUP, axis=0) * mqv for rw in rows]
    o_intra = [_dot(scores[ci].astype(BF16), v_bd[ci]) for ci in chunks]
    upd = []
    for ci in chunks:
        ke = (k[ci] * jnp.exp(e_all[ci][c:2 * c])).astype(BF16)
        m = jnp.where(same_head_vq, _dot(v_all[rows[ci]].T.astype(BF16), ke), 0.0)
        dv = GLA_DV
        upd.append(m[0:dv] + m[dv:2 * dv] + m[2 * dv:3 * dv] + m[3 * dv:4 * dv])

    st = st_scr[...]
    os_ = []
    for ci in chunks:
        b = e_all[ci][0:c]
        st_bd = jnp.concatenate([st.astype(BF16)] * GLA_GROUP, axis=0) * mvq
        os_.append(_dot_nt((q[ci] * jnp.exp(b)).astype(BF16), st_bd) + o_intra[ci])
        st = st * jnp.exp(b[c - 1:c, :]) + upd[ci]
    st_scr[...] = st

    o = jnp.concatenate(os_, axis=0)
    g = g_ref[...].astype(F32)
    gs = g * _sigmoid(g) * nw_ref[...]
    outs = []
    for h in range(GLA_GROUP):
        oh = o[:, h * GLA_DV:(h + 1) * GLA_DV]
        oh = oh * lax.rsqrt(jnp.mean(oh * oh, axis=-1, keepdims=True) + RMS_EPS)
        outs.append(oh * gs[:, h * GLA_DV:(h + 1) * GLA_DV])
    o_ref[...] = jnp.concatenate(outs, axis=1).astype(o_ref.dtype)


def _gla_call(z, wup_pad, ba_row, nw_row, p_mat, masks, w_ff1, w_ff2, layer):
    t = z.shape[0]
    tb = CHUNK * GLA_NCH
    groups = GLA_HEADS // GLA_GROUP
    nsteps = t // tb
    total = groups * nsteps
    _, d, dff = w_ff1.shape
    assert d % total == 0 and dff % total == 0
    r1, r2 = d // total, dff // total
    blk = lambda w, off: pl.BlockSpec((tb, w), lambda p, c: (c, off // w + p))
    const = lambda a: pl.BlockSpec(a.shape, lambda p, c: (0, 0))
    return pl.pallas_call(
        _gla_kernel,
        out_shape=(jax.ShapeDtypeStruct((t, GLA_V), BF16), jax.ShapeDtypeStruct((d, dff), BF16),
                   jax.ShapeDtypeStruct((dff, d), BF16)),
        grid=(groups, nsteps),
        in_specs=[
            blk(GLA_QW, C_GQ), blk(GLA_QW, C_GK), blk(GLA_VW, C_GV), blk(GLA_VW, C_GG),
            pl.BlockSpec((tb, LANES), lambda p, c: (c, C_GA // LANES)),
            pl.BlockSpec((LANES, GLA_QW), lambda p, c: (0, p)),
            pl.BlockSpec((1, GLA_QW), lambda p, c: (0, p)),
            pl.BlockSpec((1, GLA_VW), lambda p, c: (0, p)),
            const(p_mat), const(masks[0]), const(masks[1]), const(masks[2]),
            pl.BlockSpec((None, r1, dff), lambda p, c: (layer, p * nsteps + c, 0)),
            pl.BlockSpec((None, r2, d), lambda p, c: (layer, p * nsteps + c, 0)),
        ],
        out_specs=(pl.BlockSpec((tb, GLA_VW), lambda p, c: (c, p)),
                   pl.BlockSpec((r1, dff), lambda p, c: (p * nsteps + c, 0)),
                   pl.BlockSpec((r2, d), lambda p, c: (p * nsteps + c, 0))),
        scratch_shapes=[pltpu.VMEM((GLA_DV, GLA_QW), F32)],
        compiler_params=pltpu.CompilerParams(
            dimension_semantics=("arbitrary", "arbitrary"), vmem_limit_bytes=VMEM_LIMIT_BYTES),
        name="gla_mixer",
    )(z, z, z, z, z, wup_pad, ba_row, nw_row, p_mat, *masks, w_ff1, w_ff2)


RWKV_NCH = 16
RWKV_GROUP = 4
RWKV_GW = RWKV_GROUP * RWKV_HEAD
RWKV_INV_LEVELS = (2, 4, 8, 16, 32)


def _rwkv_kernel(*refs, has_vres):
    if has_vres:
        (r_ref, k_ref, v_ref, wl_ref, al_ref, gl_ref, vl_ref, vf_ref, vup_ref, v0_ref,
         w0_ref, wup_ref, a0_ref, aup_ref, gup_ref, kk_ref, ka_ref, rk_ref, gnw_ref, gnb_ref,
         tri_ref, ones_ref, wo_ref, o_ref, wob_ref, st_scr) = refs
    else:
        (r_ref, k_ref, v_ref, wl_ref, al_ref, gl_ref,
         w0_ref, wup_ref, a0_ref, aup_ref, gup_ref, kk_ref, ka_ref, rk_ref, gnw_ref, gnb_ref,
         tri_ref, ones_ref, wo_ref, o_ref, wob_ref, st_scr) = refs

    @pl.when(pl.program_id(1) == 0)
    def _():
        st_scr[...] = jnp.zeros(st_scr.shape, F32)

    wob_ref[...] = wo_ref[...].astype(BF16)

    c = CHUNK
    gw = RWKV_GW
    tt = lax.broadcasted_iota(jnp.int32, (c, gw), 0)
    jj = lax.broadcasted_iota(jnp.int32, (c, gw), 1) & (RWKV_HEAD - 1)
    strict = tt > jj
    incl = tt >= jj
    ones_bd = ones_ref[...]
    same_head_b = ones_bd != 0
    hshift = int(np.log2(RWKV_HEAD))
    same_head = ((lax.broadcasted_iota(jnp.int32, (gw, gw), 0) >> hshift)
                 == (lax.broadcasted_iota(jnp.int32, (gw, gw), 1) >> hshift))
    inv_n = 1.0 / RWKV_HEAD

    def seg_sum(x):
        return _dot_split2_rhs(x, ones_bd)

    def block_diag(x):
        return jnp.concatenate([x.astype(BF16)] * RWKV_GROUP, axis=0) * ones_bd

    tb = c * RWKV_NCH
    r_all = r_ref[...].astype(F32)
    k_all = k_ref[...].astype(F32)
    v_all = v_ref[...].astype(F32)
    w_pre = w0_ref[...] + _dot(jnp.tanh(wl_ref[...].astype(F32)).astype(BF16), wup_ref[...])
    lw_all = -jnp.exp(_log_sigmoid(w_pre) - 0.5)
    alr = _sigmoid(a0_ref[...] + _dot(al_ref[...].astype(BF16), aup_ref[...]))
    gate_all = _dot(_sigmoid(gl_ref[...].astype(F32)).astype(BF16), gup_ref[...])
    if has_vres:
        mix = _sigmoid(v0_ref[...] + _dot(vl_ref[...].astype(BF16), vup_ref[...]))
        v_all = v_all + (vf_ref[...].astype(F32) - v_all) * mix
    kk = k_all * kk_ref[...]
    k2_all = k_all * (1.0 + (alr - 1.0) * ka_ref[...])
    sums = seg_sum(jnp.concatenate([kk * kk, r_all * k2_all * rk_ref[...]], axis=0))
    kk = kk * lax.rsqrt(jnp.maximum(sums[:tb], 1e-24))
    bonus_all = sums[tb:] * v_all
    a_all = -kk
    b_all = kk * alr

    chunks = range(RWKV_NCH)
    rows = [slice(ci * c, (ci + 1) * c) for ci in chunks]
    cl = [_dot_split2_lhs(tri_ref[...], lw_all[rw]) for rw in rows]
    pre = []
    for ci in chunks:
        rw = rows[ci]
        cl_last = cl[ci][c - 1:c, :]
        e_neg = jnp.exp(-cl[ci])
        e_end = jnp.exp(cl_last - cl[ci])
        b, k2, v = b_all[rw], k2_all[rw], v_all[rw]
        at = a_all[rw] * jnp.exp(cl[ci] - lw_all[rw])
        rt = r_all[rw] * jnp.exp(cl[ci])
        pre.append(dict(
            rt=rt, v=v, at=at, g_c=jnp.exp(cl_last),
            lhs=jnp.concatenate([at, rt], axis=0).astype(BF16),
            rhs=jnp.concatenate([block_diag(b * e_neg), block_diag(k2 * e_neg)], axis=0),
            hat=jnp.concatenate([b * e_end, k2 * e_end], axis=0).astype(BF16)))
    for p in pre:
        aa = _dot_nt(p["lhs"], p["rhs"])
        p.update(
            a_ab=jnp.where(strict, aa[:c, :gw], 0.0),
            a_ak=jnp.where(strict, aa[:c, gw:], 0.0).astype(BF16),
            p_rb=jnp.where(incl, aa[c:, :gw], 0.0).astype(BF16),
            p_rk=jnp.where(incl, aa[c:, gw:], 0.0).astype(BF16))

    a_bd = [block_diag(p["a_ab"]) for p in pre]
    tinv = [jnp.where(tt == jj, 1.0, jnp.where((tt >> 1) == (jj >> 1), p["a_ab"], 0.0)) for p in pre]
    for s in RWKV_INV_LEVELS:
        shift = int(np.log2(2 * s))
        off = ((tt >> shift) == (jj >> shift)) & ((tt & s) != 0) & ((jj & s) == 0)
        t_bd = [block_diag(t) for t in tinv]
        half = [jnp.where(off, _dot(tinv[ci].astype(BF16), a_bd[ci]), 0.0).astype(BF16) for ci in chunks]
        tinv = [tinv[ci] + _dot(half[ci], t_bd[ci]) for ci in chunks]

    t_b = [t.astype(BF16) for t in tinv]
    v_bd = [block_diag(p["v"]) for p in pre]
    wt = [_dot(t_b[ci], block_diag(pre[ci]["at"])) for ci in chunks]
    akv = [block_diag(_dot(pre[ci]["a_ak"], v_bd[ci])) for ci in chunks]
    u0 = [_dot(t_b[ci], akv[ci]) for ci in chunks]
    qh = [(pre[ci]["rt"] + _dot(pre[ci]["p_rb"], block_diag(wt[ci]))).astype(BF16) for ci in chunks]
    gmat = [jnp.where(same_head, _dot(wt[ci].T.astype(BF16), pre[ci]["hat"][:c]), 0.0).astype(BF16)
            for ci in chunks]
    y0 = [_dot(pre[ci]["p_rb"], block_diag(u0[ci])) + _dot(pre[ci]["p_rk"], v_bd[ci]) for ci in chunks]
    n0c = []
    for ci in chunks:
        uv_t = jnp.concatenate([u0[ci], pre[ci]["v"]], axis=0).T.astype(BF16)
        n0 = jnp.where(same_head, _dot(uv_t, pre[ci]["hat"]), 0.0)
        n0c.append(n0[0:c] + n0[c:2 * c] + n0[2 * c:3 * c] + n0[3 * c:4 * c])

    st = st_scr[...]
    ys = []
    for ci in chunks:
        ys.append(_dot_nt(qh[ci], block_diag(st)) + y0[ci])
        st = st * pre[ci]["g_c"] + _dot(st.astype(BF16), gmat[ci]) + n0c[ci]
    st_scr[...] = st

    y = jnp.concatenate(ys, axis=0)
    yc = y - seg_sum(y) * inv_n
    var = seg_sum(yc * yc) * inv_n
    yn = yc * lax.rsqrt(var + RWKV_GN_EPS) * gnw_ref[...] + gnb_ref[...]
    o_ref[...] = ((yn + bonus_all) * gate_all).astype(o_ref.dtype)


def _rwkv_call(z, z_first, vec, mats, tri, ones_bd, w_out, layer, has_vres):
    t = z.shape[0]
    tb = CHUNK * RWKV_NCH
    gw = RWKV_GW
    groups = RWKV_HEADS // RWKV_GROUP
    nsteps = t // tb
    _, dmix, d = w_out.shape
    assert dmix % (groups * nsteps) == 0
    rows_o = dmix // (groups * nsteps)
    col = lambda off: pl.BlockSpec((tb, gw), lambda p, c: (c, off // gw + p))
    fixed = lambda off, w: pl.BlockSpec((tb, w), lambda p, c: (c, off // w))
    rowv = pl.BlockSpec((1, gw), lambda p, c: (0, p))
    upm = lambda rows: pl.BlockSpec((rows, gw), lambda p, c: (0, p))
    const = lambda a: pl.BlockSpec(a.shape, lambda p, c: (0, 0))

    in_specs = [col(C_RR), col(C_RK), col(C_RV), fixed(C_RW, LANES), fixed(C_RA, LANES), fixed(C_RG, 2 * LANES)]
    args = [z, z, z, z, z, z]
    if has_vres:
        in_specs += [fixed(C_VRES, LANES), col(C_RV), upm(LANES), rowv]
        args += [z, z_first, mats["v_up"], vec["v0"]]
    in_specs += [rowv, upm(LANES), rowv, upm(LANES), upm(2 * LANES), rowv, rowv, rowv, rowv, rowv,
                 const(tri), const(ones_bd),
                 pl.BlockSpec((None, rows_o, d), lambda p, c: (layer, p * nsteps + c, 0))]
    args += [vec["w0"], mats["w_up"], vec["a0"], mats["a_up"], mats["g_up"], vec["k_k"], vec["k_a"],
             vec["r_k"], vec["gn_w"], vec["gn_b"], tri, ones_bd, w_out]
    return pl.pallas_call(
        functools.partial(_rwkv_kernel, has_vres=has_vres),
        out_shape=(jax.ShapeDtypeStruct((t, RWKV_WIDTH), BF16), jax.ShapeDtypeStruct((dmix, d), BF16)),
        grid=(groups, nsteps),
        in_specs=in_specs,
        out_specs=(pl.BlockSpec((tb, gw), lambda p, c: (c, p)),
                   pl.BlockSpec((rows_o, d), lambda p, c: (p * nsteps + c, 0))),
        scratch_shapes=[pltpu.VMEM((CHUNK, RWKV_GW), F32)],
        compiler_params=pltpu.CompilerParams(
            dimension_semantics=("arbitrary", "arbitrary"), vmem_limit_bytes=VMEM_LIMIT_BYTES),
        name="rwkv7_mixer",
    )(*args)


OUT_TM = 512
OUT_SUB = 256


def _outproj_kernel(x_ref, oa_ref, ob_ref, wa_ref, wb_ref, g_ref, gt_ref, gpre_ref, sh_ref, sc_ref, o_ref, h_ref):
    subs = [slice(s, s + OUT_SUB) for s in range(0, OUT_TM, OUT_SUB)]
    ys = [_dot(oa_ref[rw, :], wa_ref[...]) + _dot(ob_ref[rw, :], wb_ref[...]) for rw in subs]
    post_scale = gt_ref[0] * g_ref[...]
    pre_scale = gpre_ref[...] * (1.0 + sc_ref[0])
    for rw, y in zip(subs, ys):
        ms = jnp.mean(y * y, axis=-1, keepdims=True)
        xn = x_ref[rw, :] + (y * lax.rsqrt(ms + RMS_EPS)) * post_scale
        o_ref[rw, :] = xn
        ms2 = jnp.mean(xn * xn, axis=-1, keepdims=True)
        h_ref[rw, :] = ((xn * lax.rsqrt(ms2 + RMS_EPS)) * pre_scale + sh_ref[0]).astype(h_ref.dtype)


def _outproj_call(x2, o_gla, o_rwkv, w_out_all, g_row, gpre_row, mod, layer):
    t, d = x2.shape
    assert GLA_V == RWKV_WIDTH
    return pl.pallas_call(
        _outproj_kernel,
        out_shape=(jax.ShapeDtypeStruct((t, d), F32), jax.ShapeDtypeStruct((t, d), BF16)),
        grid=(t // OUT_TM,),
        in_specs=[
            pl.BlockSpec((OUT_TM, d), lambda i: (i, 0)),
            pl.BlockSpec((OUT_TM, GLA_V), lambda i: (i, 0)),
            pl.BlockSpec((OUT_TM, RWKV_WIDTH), lambda i: (i, 0)),
            pl.BlockSpec((GLA_V, d), lambda i: (0, 0)),
            pl.BlockSpec((RWKV_WIDTH, d), lambda i: (1, 0)),
            pl.BlockSpec((1, d), lambda i: (0, 0)),
            pl.BlockSpec((1, 1, d), lambda i: (layer, 0, 2)),
            pl.BlockSpec((1, d), lambda i: (0, 0)),
            pl.BlockSpec((1, 1, d), lambda i: (layer, 0, 3)),
            pl.BlockSpec((1, 1, d), lambda i: (layer, 0, 4)),
        ],
        out_specs=(pl.BlockSpec((OUT_TM, d), lambda i: (i, 0)), pl.BlockSpec((OUT_TM, d), lambda i: (i, 0))),
        compiler_params=pltpu.CompilerParams(
            dimension_semantics=("arbitrary",), vmem_limit_bytes=VMEM_LIMIT_BYTES),
        name="outproj",
    )(x2, o_gla, o_rwkv, w_out_all, w_out_all, g_row, mod, gpre_row, mod, mod)


FFN_TM = 512
FFN_TF = 1024


def _ffn_kernel(x_ref, h_ref, w1_ref, w2_ref, gpost_ref, gt_ref, o_ref, acc_scr):
    f = pl.program_id(1)

    @pl.when(f == 0)
    def _():
        acc_scr[...] = jnp.zeros(acc_scr.shape, F32)

    u = jnp.maximum(_dot(h_ref[...], w1_ref[...]), 0.0)
    acc_scr[...] += _dot((u * u).astype(BF16), w2_ref[...])

    @pl.when(f == pl.num_programs(1) - 1)
    def _():
        y = acc_scr[...]
        ms = jnp.mean(y * y, axis=-1, keepdims=True)
        o_ref[...] = x_ref[...] + gt_ref[0] * (y * lax.rsqrt(ms + RMS_EPS) * gpost_ref[...])


def _ffn_call(x2, h2, gpost_row, mod, layer, w1, w2):
    t, d = x2.shape
    dff = w1.shape[1]
    return pl.pallas_call(
        _ffn_kernel,
        out_shape=jax.ShapeDtypeStruct((t, d), F32),
        grid=(t // FFN_TM, dff // FFN_TF),
        in_specs=[
            pl.BlockSpec((FFN_TM, d), lambda i, f: (i, 0)),
            pl.BlockSpec((FFN_TM, d), lambda i, f: (i, 0)),
            pl.BlockSpec((d, FFN_TF), lambda i, f: (0, f)),
            pl.BlockSpec((FFN_TF, d), lambda i, f: (f, 0)),
            pl.BlockSpec((1, d), lambda i, f: (0, 0)),
            pl.BlockSpec((1, 1, d), lambda i, f: (layer, 0, 5)),
        ],
        out_specs=pl.BlockSpec((FFN_TM, d), lambda i, f: (i, 0)),
        scratch_shapes=[pltpu.VMEM((FFN_TM, d), F32)],
        compiler_params=pltpu.CompilerParams(
            dimension_semantics=("arbitrary", "arbitrary"), vmem_limit_bytes=VMEM_LIMIT_BYTES),
        name="ffn",
    )(x2, h2, w1, w2, gpost_row, mod)


def _pad_cols(w, width):
    return jnp.pad(w, ((0, 0), (0, width - w.shape[1])))


def _pad_rows(w, rows):
    return jnp.pad(w, ((0, rows - w.shape[0]), (0, 0)))


def _pack_inproj(w_in_t, rwkv_mu, vres_w_down, vres_mu):
    n_layers, _, d = w_in_t.shape
    gla_cols = N_MAIN + GLA_GATE_RANK
    rkv_end = gla_cols + N_MAIN
    w_end, a_end = rkv_end + RWKV_W_RANK, rkv_end + RWKV_W_RANK + RWKV_A_RANK
    pad_to = lambda w, n: jnp.pad(w, ((0, 0), (0, n - w.shape[1]), (0, 0)))
    vres_t = jnp.concatenate(
        [jnp.zeros((1, RWKV_V_RANK, d), BF16), jnp.swapaxes(vres_w_down, 1, 2).astype(BF16)], axis=0)
    w_run2_t = w_in_t[:, gla_cols:rkv_end]
    w_extra_t = jnp.concatenate(
        [pad_to(w_in_t[:, N_MAIN:gla_cols], LANES), pad_to(vres_t, LANES), w_in_t[:, a_end:],
         pad_to(w_in_t[:, rkv_end:w_end], LANES), pad_to(w_in_t[:, w_end:a_end], LANES)], axis=1)
    vmu = jnp.concatenate([jnp.zeros((1, RWKV_V_RANK), F32), vres_mu], axis=0)
    padc = lambda m, n: jnp.pad(m, ((0, 0), (0, n - m.shape[1])))
    mu_pack = jnp.concatenate(
        [jnp.zeros((n_layers, N_MAIN), F32), rwkv_mu[:, :N_MAIN], jnp.zeros((n_layers, LANES), F32), padc(vmu, LANES),
         rwkv_mu[:, N_MAIN + RWKV_W_RANK + RWKV_A_RANK:], padc(rwkv_mu[:, N_MAIN:N_MAIN + RWKV_W_RANK], LANES),
         padc(rwkv_mu[:, N_MAIN + RWKV_W_RANK:N_MAIN + RWKV_W_RANK + RWKV_A_RANK], LANES)], axis=1)
    return w_run2_t, w_extra_t, mu_pack[:, None, :]


def kernel(x, c, w_ada, b_ada, g_pre_mix, g_post_mix, g_pre_ffn, g_post_ffn, w_in, gla_w_a_up, gla_b_a, gla_norm_w, rwkv_mu, rwkv_w0, rwkv_w_up, rwkv_a0, rwkv_a_up, rwkv_g_up, rwkv_k_k, rwkv_k_a, rwkv_r_k, rwkv_gn_w, rwkv_gn_b, vres_w_down, vres_mu, vres_up, vres_v0, w_out, w_ff1, w_ff2):
    bsz, t, d = x.shape
    assert bsz == 1 and d == D_MODEL and t % (CHUNK * max(GLA_NCH, RWKV_NCH)) == 0
    assert t % FFN_TM == 0 and t % INP_TM == 0 and t % OUT_TM == 0
    n_layers = w_ada.shape[0]

    mod = _ada_call(c.reshape(d, 1), w_ada, b_ada.reshape(n_layers, 1, 6 * d))

    p_gla = jnp.asarray(_gla_exponent_matrix(), BF16)
    gla_masks = tuple(jnp.asarray(m, BF16) for m in _gla_masks())
    tri = jnp.asarray(np.tril(np.ones((CHUNK, CHUNK), np.float32)), BF16)
    head_of_lane = np.arange(RWKV_GW) // RWKV_HEAD
    ones_bd = jnp.asarray((head_of_lane[:, None] == head_of_lane[None, :]).astype(np.float32), BF16)

    w_in_t = jnp.swapaxes(w_in, 1, 2).astype(BF16)
    w_run2_t, w_extra_t, mu_pack = _pack_inproj(w_in_t, rwkv_mu, vres_w_down, vres_mu)
    x2 = x.reshape(t, d)
    z_first = None
    for i in range(n_layers):
        j = i - 1
        z = _inproj_call(x2, g_pre_mix[i][None, :], mod, i, w_in_t, w_run2_t, w_extra_t, mu_pack[i])
        if i == 0:
            z_first = z

        o_gla, w_ff1_b, w_ff2_b = _gla_call(
            z, _pad_rows(gla_w_a_up[i], LANES).astype(BF16), gla_b_a[i][None, :], gla_norm_w[i][None, :], p_gla,
            gla_masks, w_ff1, w_ff2, i)

        row = lambda a: a.reshape(1, RWKV_WIDTH)
        vec = dict(w0=row(rwkv_w0[i]), a0=row(rwkv_a0[i]), k_k=row(rwkv_k_k[i]), k_a=row(rwkv_k_a[i]),
                   r_k=row(rwkv_r_k[i]), gn_w=row(rwkv_gn_w[i]), gn_b=row(rwkv_gn_b[i]))
        mats = dict(w_up=_pad_rows(rwkv_w_up[i], LANES).astype(BF16),
                    a_up=_pad_rows(rwkv_a_up[i], LANES).astype(BF16),
                    g_up=rwkv_g_up[i].astype(BF16))
        if i > 0:
            vec["v0"] = row(vres_v0[j])
            mats["v_up"] = _pad_rows(vres_up[j], LANES).astype(BF16)
        o_rwkv, w_out_b = _rwkv_call(z, z_first, vec, mats, tri, ones_bd, w_out, i, has_vres=i > 0)

        x2, h2 = _outproj_call(x2, o_gla, o_rwkv, w_out_b, g_post_mix[i][None, :], g_pre_ffn[i][None, :], mod, i)
        x2 = _ffn_call(x2, h2, g_post_ffn[i][None, :], mod, i, w_ff1_b, w_ff2_b)
    return x2.reshape(bsz, t, d)
```

```python
import functools

import numpy as np
import jax
import jax.numpy as jnp
from jax import lax
from jax.experimental import pallas as pl
from jax.experimental.pallas import tpu as pltpu

F32 = jnp.float32
BF16 = jnp.bfloat16

D_MODEL = 2048
DEPTH = 2
GLA_V = 1024
GLA_DV = 128
GLA_HEADS = 8
GLA_DK = 64
GLA_QK = 512
GLA_GATE_RANK = 16
GLA_TAU = 16.0
RWKV_WIDTH = 1024
RWKV_HEAD = 64
RWKV_HEADS = 16
RWKV_W_RANK = 96
RWKV_A_RANK = 96
RWKV_G_RANK = 256
RWKV_V_RANK = 64
RWKV_GN_EPS = 64e-5
D_FF = 4 * D_MODEL
RMS_EPS = 1e-6

LANES = 128
CHUNK = 64
VMEM_LIMIT_BYTES = 56 * 1024 * 1024

C_GQ = 0
C_GK = 512
C_GV = 1024
C_GG = 2048
C_RR = 3072
C_RK = 4096
C_RV = 5120
C_GA = 6144
C_VRES = 6272
C_RG = 6400
C_RW = 6656
C_RA = 6784
N_PACK = 6912
N_MAIN = 3072
N_EXTRA = N_PACK - 2 * N_MAIN


def _dot(a, b):
    return jnp.dot(a, b, preferred_element_type=F32)


def _dot_nt(a, b):
    return lax.dot_general(a, b, (((1,), (1,)), ((), ())), preferred_element_type=F32)


def _dot_split2_lhs(p, x):
    hi = x.astype(BF16)
    lo = (x - hi.astype(F32)).astype(BF16)
    y = _dot(p, jnp.concatenate([hi, lo], axis=1))
    return y[:, :x.shape[1]] + y[:, x.shape[1]:]


def _dot_split2_rhs(x, p):
    hi = x.astype(BF16)
    lo = (x - hi.astype(F32)).astype(BF16)
    y = _dot(jnp.concatenate([hi, lo], axis=0), p)
    return y[:x.shape[0]] + y[x.shape[0]:]


def _log_sigmoid(x):
    return jnp.minimum(x, 0.0) - jnp.log(1.0 + jnp.exp(-jnp.abs(x)))


def _sigmoid(x):
    return 1.0 / (1.0 + jnp.exp(-x))


ADA_TN = 1024


def _ada_kernel(c_ref, w_ref, b_ref, o_ref):
    c = c_ref[...]
    cond = c * _sigmoid(c)
    o_ref[0] = jnp.sum(w_ref[0] * cond, axis=0, keepdims=True) + b_ref[0]


def _ada_call(c_col, w_ada, b_ada3):
    n_layers, d, n = w_ada.shape
    return pl.pallas_call(
        _ada_kernel,
        out_shape=jax.ShapeDtypeStruct((n_layers, 1, n), F32),
        grid=(n_layers, n // ADA_TN),
        in_specs=[
            pl.BlockSpec((d, 1), lambda l, j: (0, 0)),
            pl.BlockSpec((1, d, ADA_TN), lambda l, j: (l, 0, j)),
            pl.BlockSpec((1, 1, ADA_TN), lambda l, j: (l, 0, j)),
        ],
        out_specs=pl.BlockSpec((1, 1, ADA_TN), lambda l, j: (l, 0, j)),
        compiler_params=pltpu.CompilerParams(
            dimension_semantics=("arbitrary", "arbitrary"), vmem_limit_bytes=VMEM_LIMIT_BYTES),
        name="ada_mod",
    )(c_col, w_ada, b_ada3)


INP_TM = 1024
INP_TN = 768
INP_SUB = 256


def _inproj_kernel(x_ref, g_ref, sh_ref, sc_ref, wa_ref, wb_ref, wc_ref, mu_ref, z_ref, h_scr, carry_scr):
    i = pl.program_id(0)
    j = pl.program_id(1)

    @pl.when(i == 0)
    def _():
        carry_scr[j] = jnp.zeros(carry_scr.shape[1:], F32)

    def finish(z):
        prev = carry_scr[j]
        row = lax.broadcasted_iota(jnp.int32, z.shape, 0)
        shifted = jnp.where(row == 0, prev, pltpu.roll(z, 1, 0))
        carry_scr[j] = z[z.shape[0] - 1:, :]
        z_ref[...] = (z + mu_ref[...] * (shifted - z)).astype(z_ref.dtype)

    @pl.when(j == 0)
    def _():
        scale = g_ref[...] * (1.0 + sc_ref[0])
        zs = []
        for s in range(0, INP_TM, INP_SUB):
            x = x_ref[s:s + INP_SUB, :]
            ms = jnp.mean(x * x, axis=-1, keepdims=True)
            h = ((x * lax.rsqrt(ms + RMS_EPS)) * scale + sh_ref[0]).astype(BF16)
            h_scr[s:s + INP_SUB, :] = h
            zs.append(_dot_nt(h, wa_ref[...]))
        finish(jnp.concatenate(zs, axis=0))

    @pl.when(j > 0)
    def _():
        n_main = N_MAIN // INP_TN
        w_tile = jnp.where(j < n_main, wa_ref[...], jnp.where(j < 2 * n_main, wb_ref[...], wc_ref[...]))
        finish(_dot_nt(h_scr[...], w_tile))


def _inproj_call(x2, g_row, mod, layer, w_in_t, w_run2_t, w_extra_t, mu_pack):
    t, d = x2.shape
    n = N_PACK
    nj = n // INP_TN
    n_main = N_MAIN // INP_TN
    assert N_MAIN % INP_TN == 0 and N_EXTRA == INP_TN
    return pl.pallas_call(
        _inproj_kernel,
        out_shape=jax.ShapeDtypeStruct((t, n), BF16),
        grid=(t // INP_TM, nj),
        in_specs=[
            pl.BlockSpec((INP_TM, d), lambda i, j: (i, 0)),
            pl.BlockSpec((1, d), lambda i, j: (0, 0)),
            pl.BlockSpec((1, 1, d), lambda i, j: (layer, 0, 0)),
            pl.BlockSpec((1, 1, d), lambda i, j: (layer, 0, 1)),
            pl.BlockSpec((None, INP_TN, d), lambda i, j: (layer, jnp.minimum(j, n_main - 1), 0)),
            pl.BlockSpec((None, INP_TN, d), lambda i, j: (layer, jnp.clip(j - n_main, 0, n_main - 1), 0)),
            pl.BlockSpec((None, INP_TN, d), lambda i, j: (layer, 0, 0)),
            pl.BlockSpec((1, INP_TN), lambda i, j: (0, j)),
        ],
        out_specs=pl.BlockSpec((INP_TM, INP_TN), lambda i, j: (i, j)),
        scratch_shapes=[pltpu.VMEM((INP_TM, d), BF16), pltpu.VMEM((nj, 1, INP_TN), F32)],
        compiler_params=pltpu.CompilerParams(
            dimension_semantics=("arbitrary", "arbitrary"), vmem_limit_bytes=VMEM_LIMIT_BYTES),
        name="inproj",
    )(x2, g_row, mod, mod, w_in_t, w_run2_t, w_extra_t, mu_pack)


GLA_NCH = 16
GLA_GROUP = 4
GLA_QW = GLA_GROUP * GLA_DK
GLA_VW = GLA_GROUP * GLA_DV
GLA_LEVELS = (32, 16, 8, 4, 2, 1)


def _gla_exponent_matrix():
    c = CHUNK
    p = np.zeros((2 * c + len(GLA_LEVELS) * c, c), np.float32)
    for i in range(c):
        p[i, : i + 1] = 1.0
        p[c + i, i + 1:] = 1.0
    for li, s in enumerate(GLA_LEVELS):
        base = 2 * c + li * c
        for i in range(c):
            m = (i // (2 * s)) * (2 * s) + s
            if i & s:
                p[base + i, m + 1: i + 1] = 1.0
            else:
                p[base + i, i + 1: m + 1] = 1.0
    return p


def _gla_masks():
    hq = np.arange(GLA_QW) // GLA_DK
    hv = np.arange(GLA_VW) // GLA_DV
    qq = (hq[:, None] == hq[None, :]).astype(np.float32)
    qv = (hq[:, None] == hv[None, :]).astype(np.float32)
    return qq, qv, qv.T.copy()


def _gla_kernel(q_ref, k_ref, v_ref, g_ref, al_ref, wup_ref, ba_ref, nw_ref, p_ref, mqq_ref, mqv_ref, mvq_ref,
                w1_ref, w2_ref, o_ref, w1b_ref, w2b_ref, st_scr):
    @pl.when(pl.program_id(1) == 0)
    def _():
        st_scr[...] = jnp.zeros(st_scr.shape, F32)

    w1b_ref[...] = w1_ref[...].astype(BF16)
    w2b_ref[...] = w2_ref[...].astype(BF16)

    c = CHUNK
    qw, vw = GLA_QW, GLA_VW
    tt = lax.broadcasted_iota(jnp.int32, (c, qw), 0)
    jj = lax.broadcasted_iota(jnp.int32, (c, qw), 1) & (GLA_DK - 1)
    mqq, mqv, mvq = mqq_ref[...], mqv_ref[...], mvq_ref[...]
    same_head_vq = ((lax.broadcasted_iota(jnp.int32, (vw, qw), 0) >> int(np.log2(GLA_DV)))
                    == (lax.broadcasted_iota(jnp.int32, (vw, qw), 1) >> int(np.log2(GLA_DK))))

    def bd_qk(x):
        return jnp.concatenate([x.astype(BF16)] * GLA_GROUP, axis=0) * mqq

    x = _dot(al_ref[...].astype(BF16), wup_ref[...]) + ba_ref[...]
    la_all = _log_sigmoid(x) * (1.0 / GLA_TAU)
    q_all = q_ref[...].astype(F32) * (GLA_DK ** -0.5)
    k_all = k_ref[...].astype(F32)
    v_all = v_ref[...].astype(F32)

    chunks = range(GLA_NCH)
    rows = [slice(ci * c, (ci + 1) * c) for ci in chunks]
    e_all = [_dot_split2_lhs(p_ref[...], la_all[rw]) for rw in rows]
    q = [q_all[rw] for rw in rows]
    k = [k_all[rw] for rw in rows]

    qk_sum = _dot_split2_rhs(q_all * k_all, mqq)
    scores = [jnp.where(tt == jj, qk_sum[rw], 0.0) for rw in rows]
    for li, s in enumerate(GLA_LEVELS):
        second = (tt & s) != 0
        shift = int(np.log2(2 * s))
        same = (tt >> shift) == (jj >> shift)
        e = [jnp.exp(e_all[ci][(2 + li) * c:(3 + li) * c]) for ci in chunks]
        qd = [jnp.where(second, q[ci] * e[ci], 0.0).astype(BF16) for ci in chunks]
        kd = [bd_qk(jnp.where(second, 0.0, k[ci] * e[ci])) for ci in chunks]
        scores = [scores[ci] + jnp.where(same, _dot_nt(qd[ci], kd[ci]), 0.0) for ci in chunks]

    v_bd = [jnp.concatenate([v_all[rw].astype(BF16)] * GLA_GROUP, axis=0) * mqv for rw in rows]
    o_intra = [_dot(scores[ci].astype(BF16), v_bd[ci]) for ci in chunks]
    upd = []
    for ci in chunks:
        ke = (k[ci] * jnp.exp(e_all[ci][c:2 * c])).astype(BF16)
        m = jnp.where(same_head_vq, _dot(v_all[rows[ci]].T.astype(BF16), ke), 0.0)
        dv = GLA_DV
        upd.append(m[0:dv] + m[dv:2 * dv] + m[2 * dv:3 * dv] + m[3 * dv:4 * dv])

    st = st_scr[...]
    os_ = []
    for ci in chunks:
        b = e_all[ci][0:c]
        st_bd = jnp.concatenate([st.astype(BF16)] * GLA_GROUP, axis=0) * mvq
        os_.append(_dot_nt((q[ci] * jnp.exp(b)).astype(BF16), st_bd) + o_intra[ci])
        st = st * jnp.exp(b[c - 1:c, :]) + upd[ci]
    st_scr[...] = st

    o = jnp.concatenate(os_, axis=0)
    g = g_ref[...].astype(F32)
    gs = g * _sigmoid(g) * nw_ref[...]
    outs = []
    for h in range(GLA_GROUP):
        oh = o[:, h * GLA_DV:(h + 1) * GLA_DV]
        oh = oh * lax.rsqrt(jnp.mean(oh * oh, axis=-1, keepdims=True) + RMS_EPS)
        outs.append(oh * gs[:, h * GLA_DV:(h + 1) * GLA_DV])
    o_ref[...] = jnp.concatenate(outs, axis=1).astype(o_ref.dtype)


def _gla_call(z, wup_pad, ba_row, nw_row, p_mat, masks, w_ff1, w_ff2, layer):
    t = z.shape[0]
    tb = CHUNK * GLA_NCH
    groups = GLA_HEADS // GLA_GROUP
    nsteps = t // tb
    total = groups * nsteps
    _, d, dff = w_ff1.shape
    assert d % total == 0 and dff % total == 0
    r1, r2 = d // total, dff // total
    blk = lambda w, off: pl.BlockSpec((tb, w), lambda p, c: (c, off // w + p))
    const = lambda a: pl.BlockSpec(a.shape, lambda p, c: (0, 0))
    return pl.pallas_call(
        _gla_kernel,
        out_shape=(jax.ShapeDtypeStruct((t, GLA_V), BF16), jax.ShapeDtypeStruct((d, dff), BF16),
                   jax.ShapeDtypeStruct((dff, d), BF16)),
        grid=(groups, nsteps),
        in_specs=[
            blk(GLA_QW, C_GQ), blk(GLA_QW, C_GK), blk(GLA_VW, C_GV), blk(GLA_VW, C_GG),
            pl.BlockSpec((tb, LANES), lambda p, c: (c, C_GA // LANES)),
            pl.BlockSpec((LANES, GLA_QW), lambda p, c: (0, p)),
            pl.BlockSpec((1, GLA_QW), lambda p, c: (0, p)),
            pl.BlockSpec((1, GLA_VW), lambda p, c: (0, p)),
            const(p_mat), const(masks[0]), const(masks[1]), const(masks[2]),
            pl.BlockSpec((None, r1, dff), lambda p, c: (layer, p * nsteps + c, 0)),
            pl.BlockSpec((None, r2, d), lambda p, c: (layer, p * nsteps + c, 0)),
        ],
        out_specs=(pl.BlockSpec((tb, GLA_VW), lambda p, c: (c, p)),
                   pl.BlockSpec((r1, dff), lambda p, c: (p * nsteps + c, 0)),
                   pl.BlockSpec((r2, d), lambda p, c: (p * nsteps + c, 0))),
        scratch_shapes=[pltpu.VMEM((GLA_DV, GLA_QW), F32)],
        compiler_params=pltpu.CompilerParams(
            dimension_semantics=("arbitrary", "arbitrary"), vmem_limit_bytes=VMEM_LIMIT_BYTES),
        name="gla_mixer",
    )(z, z, z, z, z, wup_pad, ba_row, nw_row, p_mat, *masks, w_ff1, w_ff2)


RWKV_NCH = 16
RWKV_GROUP = 4
RWKV_GW = RWKV_GROUP * RWKV_HEAD
RWKV_INV_LEVELS = (2, 4, 8, 16, 32)


def _rwkv_kernel(*refs, has_vres):
    if has_vres:
        (r_ref, k_ref, v_ref, wl_ref, al_ref, gl_ref, vl_ref, vf_ref, vup_ref, v0_ref,
         w0_ref, wup_ref, a0_ref, aup_ref, gup_ref, kk_ref, ka_ref, rk_ref, gnw_ref, gnb_ref,
         tri_ref, ones_ref, wo_ref, o_ref, wob_ref, st_scr) = refs
    else:
        (r_ref, k_ref, v_ref, wl_ref, al_ref, gl_ref,
         w0_ref, wup_ref, a0_ref, aup_ref, gup_ref, kk_ref, ka_ref, rk_ref, gnw_ref, gnb_ref,
         tri_ref, ones_ref, wo_ref, o_ref, wob_ref, st_scr) = refs

    @pl.when(pl.program_id(1) == 0)
    def _():
        st_scr[...] = jnp.zeros(st_scr.shape, F32)

    wob_ref[...] = wo_ref[...].astype(BF16)

    c = CHUNK
    gw = RWKV_GW
    tt = lax.broadcasted_iota(jnp.int32, (c, gw), 0)
    jj = lax.broadcasted_iota(jnp.int32, (c, gw), 1) & (RWKV_HEAD - 1)
    strict = tt > jj
    incl = tt >= jj
    ones_bd = ones_ref[...]
    same_head_b = ones_bd != 0
    hshift = int(np.log2(RWKV_HEAD))
    same_head = ((lax.broadcasted_iota(jnp.int32, (gw, gw), 0) >> hshift)
                 == (lax.broadcasted_iota(jnp.int32, (gw, gw), 1) >> hshift))
    inv_n = 1.0 / RWKV_HEAD

    def seg_sum(x):
        return _dot_split2_rhs(x, ones_bd)

    def block_diag(x):
        return jnp.concatenate([x.astype(BF16)] * RWKV_GROUP, axis=0) * ones_bd

    tb = c * RWKV_NCH
    r_all = r_ref[...].astype(F32)
    k_all = k_ref[...].astype(F32)
    v_all = v_ref[...].astype(F32)
    w_pre = w0_ref[...] + _dot(jnp.tanh(wl_ref[...].astype(F32)).astype(BF16), wup_ref[...])
    lw_all = -jnp.exp(_log_sigmoid(w_pre) - 0.5)
    alr = _sigmoid(a0_ref[...] + _dot(al_ref[...].astype(BF16), aup_ref[...]))
    gate_all = _dot(_sigmoid(gl_ref[...].astype(F32)).astype(BF16), gup_ref[...])
    if has_vres:
        mix = _sigmoid(v0_ref[...] + _dot(vl_ref[...].astype(BF16), vup_ref[...]))
        v_all = v_all + (vf_ref[...].astype(F32) - v_all) * mix
    kk = k_all * kk_ref[...]
    k2_all = k_all * (1.0 + (alr - 1.0) * ka_ref[...])
    sums = seg_sum(jnp.concatenate([kk * kk, r_all * k2_all * rk_ref[...]], axis=0))
    kk = kk * lax.rsqrt(jnp.maximum(sums[:tb], 1e-24))
    bonus_all = sums[tb:] * v_all
    a_all = -kk
    b_all = kk * alr

    chunks = range(RWKV_NCH)
    rows = [slice(ci * c, (ci + 1) * c) for ci in chunks]
    cl = [_dot_split2_lhs(tri_ref[...], lw_all[rw]) for rw in rows]
    pre = []
    for ci in chunks:
        rw = rows[ci]
        cl_last = cl[ci][c - 1:c, :]
        e_neg = jnp.exp(-cl[ci])
        e_end = jnp.exp(cl_last - cl[ci])
        b, k2, v = b_all[rw], k2_all[rw], v_all[rw]
        at = a_all[rw] * jnp.exp(cl[ci] - lw_all[rw])
        rt = r_all[rw] * jnp.exp(cl[ci])
        pre.append(dict(
            rt=rt, v=v, at=at, g_c=jnp.exp(cl_last),
            lhs=jnp.concatenate([at, rt], axis=0).astype(BF16),
            rhs=jnp.concatenate([block_diag(b * e_neg), block_diag(k2 * e_neg)], axis=0),
            hat=jnp.concatenate([b * e_end, k2 * e_end], axis=0).astype(BF16)))
    for p in pre:
        aa = _dot_nt(p["lhs"], p["rhs"])
        p.update(
            a_ab=jnp.where(strict, aa[:c, :gw], 0.0),
            a_ak=jnp.where(strict, aa[:c, gw:], 0.0).astype(BF16),
            p_rb=jnp.where(incl, aa[c:, :gw], 0.0).astype(BF16),
            p_rk=jnp.where(incl, aa[c:, gw:], 0.0).astype(BF16))

    a_bd = [block_diag(p["a_ab"]) for p in pre]
    tinv = [jnp.where(tt == jj, 1.0, jnp.where((tt >> 1) == (jj >> 1), p["a_ab"], 0.0)) for p in pre]
    for s in RWKV_INV_LEVELS:
        shift = int(np.log2(2 * s))
        off = ((tt >> shift) == (jj >> shift)) & ((tt & s) != 0) & ((jj & s) == 0)
        t_bd = [block_diag(t) for t in tinv]
        half = [jnp.where(off, _dot(tinv[ci].astype(BF16), a_bd[ci]), 0.0).astype(BF16) for ci in chunks]
        tinv = [tinv[ci] + _dot(half[ci], t_bd[ci]) for ci in chunks]

    t_b = [t.astype(BF16) for t in tinv]
    v_bd = [block_diag(p["v"]) for p in pre]
    wt = [_dot(t_b[ci], block_diag(pre[ci]["at"])) for ci in chunks]
    akv = [block_diag(_dot(pre[ci]["a_ak"], v_bd[ci])) for ci in chunks]
    u0 = [_dot(t_b[ci], akv[ci]) for ci in chunks]
    qh = [(pre[ci]["rt"] + _dot(pre[ci]["p_rb"], block_diag(wt[ci]))).astype(BF16) for ci in chunks]
    gmat = [jnp.where(same_head, _dot(wt[ci].T.astype(BF16), pre[ci]["hat"][:c]), 0.0).astype(BF16)
            for ci in chunks]
    y0 = [_dot(pre[ci]["p_rb"], block_diag(u0[ci])) + _dot(pre[ci]["p_rk"], v_bd[ci]) for ci in chunks]
    n0c = []
    for ci in chunks:
        uv_t = jnp.concatenate([u0[ci], pre[ci]["v"]], axis=0).T.astype(BF16)
        n0 = jnp.where(same_head, _dot(uv_t, pre[ci]["hat"]), 0.0)
        n0c.append(n0[0:c] + n0[c:2 * c] + n0[2 * c:3 * c] + n0[3 * c:4 * c])

    st = st_scr[...]
    ys = []
    for ci in chunks:
        ys.append(_dot_nt(qh[ci], block_diag(st)) + y0[ci])
        st = st * pre[ci]["g_c"] + _dot(st.astype(BF16), gmat[ci]) + n0c[ci]
    st_scr[...] = st

    y = jnp.concatenate(ys, axis=0)
    yc = y - seg_sum(y) * inv_n
    var = seg_sum(yc * yc) * inv_n
    yn = yc * lax.rsqrt(var + RWKV_GN_EPS) * gnw_ref[...] + gnb_ref[...]
    o_ref[...] = ((yn + bonus_all) * gate_all).astype(o_ref.dtype)


def _rwkv_call(z, z_first, vec, mats, tri, ones_bd, w_out, layer, has_vres):
    t = z.shape[0]
    tb = CHUNK * RWKV_NCH
    gw = RWKV_GW
    groups = RWKV_HEADS // RWKV_GROUP
    nsteps = t // tb
    _, dmix, d = w_out.shape
    assert dmix % (groups * nsteps) == 0
    rows_o = dmix // (groups * nsteps)
    col = lambda off: pl.BlockSpec((tb, gw), lambda p, c: (c, off // gw + p))
    fixed = lambda off, w: pl.BlockSpec((tb, w), lambda p, c: (c, off // w))
    rowv = pl.BlockSpec((1, gw), lambda p, c: (0, p))
    upm = lambda rows: pl.BlockSpec((rows, gw), lambda p, c: (0, p))
    const = lambda a: pl.BlockSpec(a.shape, lambda p, c: (0, 0))

    in_specs = [col(C_RR), col(C_RK), col(C_RV), fixed(C_RW, LANES), fixed(C_RA, LANES), fixed(C_RG, 2 * LANES)]
    args = [z, z, z, z, z, z]
    if has_vres:
        in_specs += [fixed(C_VRES, LANES), col(C_RV), upm(LANES), rowv]
        args += [z, z_first, mats["v_up"], vec["v0"]]
    in_specs += [rowv, upm(LANES), rowv, upm(LANES), upm(2 * LANES), rowv, rowv, rowv, rowv, rowv,
                 const(tri), const(ones_bd),
                 pl.BlockSpec((None, rows_o, d), lambda p, c: (layer, p * nsteps + c, 0))]
    args += [vec["w0"], mats["w_up"], vec["a0"], mats["a_up"], mats["g_up"], vec["k_k"], vec["k_a"],
             vec["r_k"], vec["gn_w"], vec["gn_b"], tri, ones_bd, w_out]
    return pl.pallas_call(
        functools.partial(_rwkv_kernel, has_vres=has_vres),
        out_shape=(jax.ShapeDtypeStruct((t, RWKV_WIDTH), BF16), jax.ShapeDtypeStruct((dmix, d), BF16)),
        grid=(groups, nsteps),
        in_specs=in_specs,
        out_specs=(pl.BlockSpec((tb, gw), lambda p, c: (c, p)),
                   pl.BlockSpec((rows_o, d), lambda p, c: (p * nsteps + c, 0))),
        scratch_shapes=[pltpu.VMEM((CHUNK, RWKV_GW), F32)],
        compiler_params=pltpu.CompilerParams(
            dimension_semantics=("arbitrary", "arbitrary"), vmem_limit_bytes=VMEM_LIMIT_BYTES),
        name="rwkv7_mixer",
    )(*args)


OUT_TM = 512
OUT_SUB = 128


def _outproj_kernel(x_ref, oa_ref, ob_ref, wa_ref, wb_ref, g_ref, gt_ref, gpre_ref, sh_ref, sc_ref, o_ref, h_ref):
    subs = [slice(s, s + OUT_SUB) for s in range(0, OUT_TM, OUT_SUB)]
    ys = [_dot(oa_ref[rw, :], wa_ref[...]) + _dot(ob_ref[rw, :], wb_ref[...]) for rw in subs]
    post_scale = gt_ref[0] * g_ref[...]
    pre_scale = gpre_ref[...] * (1.0 + sc_ref[0])
    for rw, y in zip(subs, ys):
        ms = jnp.mean(y * y, axis=-1, keepdims=True)
        xn = x_ref[rw, :] + (y * lax.rsqrt(ms + RMS_EPS)) * post_scale
        o_ref[rw, :] = xn
        ms2 = jnp.mean(xn * xn, axis=-1, keepdims=True)
        h_ref[rw, :] = ((xn * lax.rsqrt(ms2 + RMS_EPS)) * pre_scale + sh_ref[0]).astype(h_ref.dtype)


def _outproj_call(x2, o_gla, o_rwkv, w_out_all, g_row, gpre_row, mod, layer):
    t, d = x2.shape
    assert GLA_V == RWKV_WIDTH
    return pl.pallas_call(
        _outproj_kernel,
        out_shape=(jax.ShapeDtypeStruct((t, d), F32), jax.ShapeDtypeStruct((t, d), BF16)),
        grid=(t // OUT_TM,),
        in_specs=[
            pl.BlockSpec((OUT_TM, d), lambda i: (i, 0)),
            pl.BlockSpec((OUT_TM, GLA_V), lambda i: (i, 0)),
            pl.BlockSpec((OUT_TM, RWKV_WIDTH), lambda i: (i, 0)),
            pl.BlockSpec((GLA_V, d), lambda i: (0, 0)),
            pl.BlockSpec((RWKV_WIDTH, d), lambda i: (1, 0)),
            pl.BlockSpec((1, d), lambda i: (0, 0)),
            pl.BlockSpec((1, 1, d), lambda i: (layer, 0, 2)),
            pl.BlockSpec((1, d), lambda i: (0, 0)),
            pl.BlockSpec((1, 1, d), lambda i: (layer, 0, 3)),
            pl.BlockSpec((1, 1, d), lambda i: (layer, 0, 4)),
        ],
        out_specs=(pl.BlockSpec((OUT_TM, d), lambda i: (i, 0)), pl.BlockSpec((OUT_TM, d), lambda i: (i, 0))),
        compiler_params=pltpu.CompilerParams(
            dimension_semantics=("arbitrary",), vmem_limit_bytes=VMEM_LIMIT_BYTES),
        name="outproj",
    )(x2, o_gla, o_rwkv, w_out_all, w_out_all, g_row, mod, gpre_row, mod, mod)


FFN_TM = 512
FFN_TF = 1024
FFN_SUB = 256


def _ffn_kernel(x_ref, h_ref, w1_ref, w2_ref, gpost_ref, gt_ref, o_ref, acc_scr):
    f = pl.program_id(1)
    last = pl.num_programs(1) - 1

    def partial(rows):
        u = jnp.maximum(_dot(h_ref[rows, :], w1_ref[...]), 0.0)
        return _dot((u * u).astype(BF16), w2_ref[...])

    everything = slice(0, FFN_TM)

    @pl.when(f == 0)
    def _():
        acc_scr[...] = partial(everything)

    @pl.when((f > 0) & (f < last))
    def _():
        acc_scr[...] += partial(everything)

    @pl.when(f == last)
    def _():
        post_scale = gt_ref[0] * gpost_ref[...]
        for s in range(0, FFN_TM, FFN_SUB):
            rows = slice(s, s + FFN_SUB)
            y = acc_scr[rows, :] + partial(rows)
            ms = jnp.mean(y * y, axis=-1, keepdims=True)
            o_ref[rows, :] = x_ref[rows, :] + (y * lax.rsqrt(ms + RMS_EPS)) * post_scale


def _ffn_call(x2, h2, gpost_row, mod, layer, w1, w2):
    t, d = x2.shape
    dff = w1.shape[1]
    return pl.pallas_call(
        _ffn_kernel,
        out_shape=jax.ShapeDtypeStruct((t, d), F32),
        grid=(t // FFN_TM, dff // FFN_TF),
        in_specs=[
            pl.BlockSpec((FFN_TM, d), lambda i, f: (i, 0)),
            pl.BlockSpec((FFN_TM, d), lambda i, f: (i, 0)),
            pl.BlockSpec((d, FFN_TF), lambda i, f: (0, f)),
            pl.BlockSpec((FFN_TF, d), lambda i, f: (f, 0)),
            pl.BlockSpec((1, d), lambda i, f: (0, 0)),
            pl.BlockSpec((1, 1, d), lambda i, f: (layer, 0, 5)),
        ],
        out_specs=pl.BlockSpec((FFN_TM, d), lambda i, f: (i, 0)),
        scratch_shapes=[pltpu.VMEM((FFN_TM, d), F32)],
        compiler_params=pltpu.CompilerParams(
            dimension_semantics=("arbitrary", "arbitrary"), vmem_limit_bytes=VMEM_LIMIT_BYTES),
        name="ffn",
    )(x2, h2, w1, w2, gpost_row, mod)


def _pad_cols(w, width):
    return jnp.pad(w, ((0, 0), (0, width - w.shape[1])))


def _pad_rows(w, rows):
    return jnp.pad(w, ((0, rows - w.shape[0]), (0, 0)))


def _pack_inproj(w_in_t, rwkv_mu, vres_w_down, vres_mu):
    n_layers, _, d = w_in_t.shape
    gla_cols = N_MAIN + GLA_GATE_RANK
    rkv_end = gla_cols + N_MAIN
    w_end, a_end = rkv_end + RWKV_W_RANK, rkv_end + RWKV_W_RANK + RWKV_A_RANK
    pad_to = lambda w, n: jnp.pad(w, ((0, 0), (0, n - w.shape[1]), (0, 0)))
    vres_t = jnp.concatenate(
        [jnp.zeros((1, RWKV_V_RANK, d), BF16), jnp.swapaxes(vres_w_down, 1, 2).astype(BF16)], axis=0)
    w_run2_t = w_in_t[:, gla_cols:rkv_end]
    w_extra_t = jnp.concatenate(
        [pad_to(w_in_t[:, N_MAIN:gla_cols], LANES), pad_to(vres_t, LANES), w_in_t[:, a_end:],
         pad_to(w_in_t[:, rkv_end:w_end], LANES), pad_to(w_in_t[:, w_end:a_end], LANES)], axis=1)
    vmu = jnp.concatenate([jnp.zeros((1, RWKV_V_RANK), F32), vres_mu], axis=0)
    padc = lambda m, n: jnp.pad(m, ((0, 0), (0, n - m.shape[1])))
    mu_pack = jnp.concatenate(
        [jnp.zeros((n_layers, N_MAIN), F32), rwkv_mu[:, :N_MAIN], jnp.zeros((n_layers, LANES), F32), padc(vmu, LANES),
         rwkv_mu[:, N_MAIN + RWKV_W_RANK + RWKV_A_RANK:], padc(rwkv_mu[:, N_MAIN:N_MAIN + RWKV_W_RANK], LANES),
         padc(rwkv_mu[:, N_MAIN + RWKV_W_RANK:N_MAIN + RWKV_W_RANK + RWKV_A_RANK], LANES)], axis=1)
    return w_run2_t, w_extra_t, mu_pack[:, None, :]


def kernel(x, c, w_ada, b_ada, g_pre_mix, g_post_mix, g_pre_ffn, g_post_ffn, w_in, gla_w_a_up, gla_b_a, gla_norm_w, rwkv_mu, rwkv_w0, rwkv_w_up, rwkv_a0, rwkv_a_up, rwkv_g_up, rwkv_k_k, rwkv_k_a, rwkv_r_k, rwkv_gn_w, rwkv_gn_b, vres_w_down, vres_mu, vres_up, vres_v0, w_out, w_ff1, w_ff2):
    bsz, t, d = x.shape
    assert bsz == 1 and d == D_MODEL and t % (CHUNK * max(GLA_NCH, RWKV_NCH)) == 0
    assert t % FFN_TM == 0 and t % INP_TM == 0 and t % OUT_TM == 0
    n_layers = w_ada.shape[0]

    mod = _ada_call(c.reshape(d, 1), w_ada, b_ada.reshape(n_layers, 1, 6 * d))

    p_gla = jnp.asarray(_gla_exponent_matrix(), BF16)
    gla_masks = tuple(jnp.asarray(m, BF16) for m in _gla_masks())
    tri = jnp.asarray(np.tril(np.ones((CHUNK, CHUNK), np.float32)), BF16)
    head_of_lane = np.arange(RWKV_GW) // RWKV_HEAD
    ones_bd = jnp.asarray((head_of_lane[:, None] == head_of_lane[None, :]).astype(np.float32), BF16)

    w_in_t = jnp.swapaxes(w_in, 1, 2).astype(BF16)
    w_run2_t, w_extra_t, mu_pack = _pack_inproj(w_in_t, rwkv_mu, vres_w_down, vres_mu)
    x2 = x.reshape(t, d)
    z_first = None
    for i in range(n_layers):
        j = i - 1
        z = _inproj_call(x2, g_pre_mix[i][None, :], mod, i, w_in_t, w_run2_t, w_extra_t, mu_pack[i])
        if i == 0:
            z_first = z

        o_gla, w_ff1_b, w_ff2_b = _gla_call(
            z, _pad_rows(gla_w_a_up[i], LANES).astype(BF16), gla_b_a[i][None, :], gla_norm_w[i][None, :], p_gla,
            gla_masks, w_ff1, w_ff2, i)

        row = lambda a: a.reshape(1, RWKV_WIDTH)
        vec = dict(w0=row(rwkv_w0[i]), a0=row(rwkv_a0[i]), k_k=row(rwkv_k_k[i]), k_a=row(rwkv_k_a[i]),
                   r_k=row(rwkv_r_k[i]), gn_w=row(rwkv_gn_w[i]), gn_b=row(rwkv_gn_b[i]))
        mats = dict(w_up=_pad_rows(rwkv_w_up[i], LANES).astype(BF16),
                    a_up=_pad_rows(rwkv_a_up[i], LANES).astype(BF16),
                    g_up=rwkv_g_up[i].astype(BF16))
        if i > 0:
            vec["v0"] = row(vres_v0[j])
            mats["v_up"] = _pad_rows(vres_up[j], LANES).astype(BF16)
        o_rwkv, w_out_b = _rwkv_call(z, z_first, vec, mats, tri, ones_bd, w_out, i, has_vres=i > 0)

        x2, h2 = _outproj_call(x2, o_gla, o_rwkv, w_out_b, g_post_mix[i][None, :], g_pre_ffn[i][None, :], mod, i)
        x2 = _ffn_call(x2, h2, g_post_ffn[i][None, :], mod, i, w_ff1_b, w_ff2_b)
    return x2.reshape(bsz, t, d)
```

```python
import functools

import numpy as np
import jax
import jax.numpy as jnp
from jax import lax
from jax.experimental import pallas as pl
from jax.experimental.pallas import tpu as pltpu

F32 = jnp.float32
BF16 = jnp.bfloat16

D_MODEL = 2048
DEPTH = 2
GLA_V = 1024
GLA_DV = 128
GLA_HEADS = 8
GLA_DK = 64
GLA_QK = 512
GLA_GATE_RANK = 16
GLA_TAU = 16.0
RWKV_WIDTH = 1024
RWKV_HEAD = 64
RWKV_HEADS = 16
RWKV_W_RANK = 96
RWKV_A_RANK = 96
RWKV_G_RANK = 256
RWKV_V_RANK = 64
RWKV_GN_EPS = 64e-5
D_FF = 4 * D_MODEL
RMS_EPS = 1e-6

LANES = 128
CHUNK = 64
VMEM_LIMIT_BYTES = 56 * 1024 * 1024

C_GQ = 0
C_GK = 512
C_GV = 1024
C_GG = 2048
C_RR = 3072
C_RK = 4096
C_RV = 5120
C_GA = 6144
C_VRES = 6272
C_RG = 6400
C_RW = 6656
C_RA = 6784
N_PACK = 6912
N_MAIN = 3072
N_EXTRA = N_PACK - 2 * N_MAIN


def _dot(a, b):
    return jnp.dot(a, b, preferred_element_type=F32)


def _dot_nt(a, b):
    return lax.dot_general(a, b, (((1,), (1,)), ((), ())), preferred_element_type=F32)


def _dot_split2_lhs(p, x):
    hi = x.astype(BF16)
    lo = (x - hi.astype(F32)).astype(BF16)
    y = _dot(p, jnp.concatenate([hi, lo], axis=1))
    return y[:, :x.shape[1]] + y[:, x.shape[1]:]


def _dot_split2_rhs(x, p):
    hi = x.astype(BF16)
    lo = (x - hi.astype(F32)).astype(BF16)
    y = _dot(jnp.concatenate([hi, lo], axis=0), p)
    return y[:x.shape[0]] + y[x.shape[0]:]


def _log_sigmoid(x):
    return jnp.minimum(x, 0.0) - jnp.log(1.0 + jnp.exp(-jnp.abs(x)))


def _sigmoid(x):
    return 1.0 / (1.0 + jnp.exp(-x))


ADA_TN = 1024


def _ada_kernel(c_ref, w_ref, b_ref, o_ref):
    c = c_ref[...]
    cond = c * _sigmoid(c)
    o_ref[0] = jnp.sum(w_ref[0] * cond, axis=0, keepdims=True) + b_ref[0]


def _ada_call(c_col, w_ada, b_ada3):
    n_layers, d, n = w_ada.shape
    return pl.pallas_call(
        _ada_kernel,
        out_shape=jax.ShapeDtypeStruct((n_layers, 1, n), F32),
        grid=(n_layers, n // ADA_TN),
        in_specs=[
            pl.BlockSpec((d, 1), lambda l, j: (0, 0)),
            pl.BlockSpec((1, d, ADA_TN), lambda l, j: (l, 0, j)),
            pl.BlockSpec((1, 1, ADA_TN), lambda l, j: (l, 0, j)),
        ],
        out_specs=pl.BlockSpec((1, 1, ADA_TN), lambda l, j: (l, 0, j)),
        compiler_params=pltpu.CompilerParams(
            dimension_semantics=("arbitrary", "arbitrary"), vmem_limit_bytes=VMEM_LIMIT_BYTES),
        name="ada_mod",
    )(c_col, w_ada, b_ada3)


INP_TM = 1024
INP_TN = 768
INP_SUB = 256


def _inproj_kernel(x_ref, g_ref, sh_ref, sc_ref, wa_ref, wb_ref, wc_ref, mu_ref, z_ref, h_scr, carry_scr):
    i = pl.program_id(0)
    j = pl.program_id(1)

    @pl.when(i == 0)
    def _():
        carry_scr[j] = jnp.zeros(carry_scr.shape[1:], F32)

    def finish(z):
        prev = carry_scr[j]
        row = lax.broadcasted_iota(jnp.int32, z.shape, 0)
        shifted = jnp.where(row == 0, prev, pltpu.roll(z, 1, 0))
        carry_scr[j] = z[z.shape[0] - 1:, :]
        z_ref[...] = (z + mu_ref[...] * (shifted - z)).astype(z_ref.dtype)

    @pl.when(j == 0)
    def _():
        scale = g_ref[...] * (1.0 + sc_ref[0])
        zs = []
        for s in range(0, INP_TM, INP_SUB):
            x = x_ref[s:s + INP_SUB, :]
            ms = jnp.mean(x * x, axis=-1, keepdims=True)
            h = ((x * lax.rsqrt(ms + RMS_EPS)) * scale + sh_ref[0]).astype(BF16)
            h_scr[s:s + INP_SUB, :] = h
            zs.append(_dot_nt(h, wa_ref[...]))
        finish(jnp.concatenate(zs, axis=0))

    @pl.when(j > 0)
    def _():
        n_main = N_MAIN // INP_TN
        w_tile = jnp.where(j < n_main, wa_ref[...], jnp.where(j < 2 * n_main, wb_ref[...], wc_ref[...]))
        finish(_dot_nt(h_scr[...], w_tile))


def _inproj_call(x2, g_row, mod, layer, w_in_t, w_run2_t, w_extra_t, mu_pack):
    t, d = x2.shape
    n = N_PACK
    nj = n // INP_TN
    n_main = N_MAIN // INP_TN
    assert N_MAIN % INP_TN == 0 and N_EXTRA == INP_TN
    return pl.pallas_call(
        _inproj_kernel,
        out_shape=jax.ShapeDtypeStruct((t, n), BF16),
        grid=(t // INP_TM, nj),
        in_specs=[
            pl.BlockSpec((INP_TM, d), lambda i, j: (i, 0)),
            pl.BlockSpec((1, d), lambda i, j: (0, 0)),
            pl.BlockSpec((1, 1, d), lambda i, j: (layer, 0, 0)),
            pl.BlockSpec((1, 1, d), lambda i, j: (layer, 0, 1)),
            pl.BlockSpec((None, INP_TN, d), lambda i, j: (layer, jnp.minimum(j, n_main - 1), 0)),
            pl.BlockSpec((None, INP_TN, d), lambda i, j: (layer, jnp.clip(j - n_main, 0, n_main - 1), 0)),
            pl.BlockSpec((None, INP_TN, d), lambda i, j: (layer, 0, 0)),
            pl.BlockSpec((1, INP_TN), lambda i, j: (0, j)),
        ],
        out_specs=pl.BlockSpec((INP_TM, INP_TN), lambda i, j: (i, j)),
        scratch_shapes=[pltpu.VMEM((INP_TM, d), BF16), pltpu.VMEM((nj, 1, INP_TN), F32)],
        compiler_params=pltpu.CompilerParams(
            dimension_semantics=("arbitrary", "arbitrary"), vmem_limit_bytes=VMEM_LIMIT_BYTES),
        name="inproj",
    )(x2, g_row, mod, mod, w_in_t, w_run2_t, w_extra_t, mu_pack)


GLA_NCH = 16
GLA_GROUP = 4
GLA_QW = GLA_GROUP * GLA_DK
GLA_VW = GLA_GROUP * GLA_DV
GLA_LEVELS = (32, 16, 8, 4, 2, 1)


def _gla_exponent_matrix():
    c = CHUNK
    p = np.zeros((2 * c + len(GLA_LEVELS) * c, c), np.float32)
    for i in range(c):
        p[i, : i + 1] = 1.0
        p[c + i, i + 1:] = 1.0
    for li, s in enumerate(GLA_LEVELS):
        base = 2 * c + li * c
        for i in range(c):
            m = (i // (2 * s)) * (2 * s) + s
            if i & s:
                p[base + i, m + 1: i + 1] = 1.0
            else:
                p[base + i, i + 1: m + 1] = 1.0
    return p


def _gla_masks():
    hq = np.arange(GLA_QW) // GLA_DK
    hv = np.arange(GLA_VW) // GLA_DV
    qq = (hq[:, None] == hq[None, :]).astype(np.float32)
    qv = (hq[:, None] == hv[None, :]).astype(np.float32)
    return qq, qv, qv.T.copy()


def _gla_kernel(q_ref, k_ref, v_ref, g_ref, al_ref, wup_ref, ba_ref, nw_ref, p_ref, mqq_ref, mqv_ref, mvq_ref,
                w1_ref, w2_ref, o_ref, w1b_ref, w2b_ref, st_scr):
    @pl.when(pl.program_id(1) == 0)
    def _():
        st_scr[...] = jnp.zeros(st_scr.shape, F32)

    w1b_ref[...] = w1_ref[...].astype(BF16)
    w2b_ref[...] = w2_ref[...].astype(BF16)

    c = CHUNK
    qw, vw = GLA_QW, GLA_VW
    tt = lax.broadcasted_iota(jnp.int32, (c, qw), 0)
    jj = lax.broadcasted_iota(jnp.int32, (c, qw), 1) & (GLA_DK - 1)
    mqq, mqv, mvq = mqq_ref[...], mqv_ref[...], mvq_ref[...]
    same_head_vq = ((lax.broadcasted_iota(jnp.int32, (vw, qw), 0) >> int(np.log2(GLA_DV)))
                    == (lax.broadcasted_iota(jnp.int32, (vw, qw), 1) >> int(np.log2(GLA_DK))))

    def bd_qk(x):
        return jnp.concatenate([x.astype(BF16)] * GLA_GROUP, axis=0) * mqq

    x = _dot(al_ref[...].astype(BF16), wup_ref[...]) + ba_ref[...]
    la_all = _log_sigmoid(x) * (1.0 / GLA_TAU)
    q_all = q_ref[...].astype(F32) * (GLA_DK ** -0.5)
    k_all = k_ref[...].astype(F32)
    v_all = v_ref[...].astype(F32)

    chunks = range(GLA_NCH)
    rows = [slice(ci * c, (ci + 1) * c) for ci in chunks]
    e_all = [_dot_split2_lhs(p_ref[...], la_all[rw]) for rw in rows]
    q = [q_all[rw] for rw in rows]
    k = [k_all[rw] for rw in rows]

    qk_sum = _dot_split2_rhs(q_all * k_all, mqq)
    scores = [jnp.where(tt == jj, qk_sum[rw], 0.0) for rw in rows]
    for li, s in enumerate(GLA_LEVELS):
        second = (tt & s) != 0
        shift = int(np.log2(2 * s))
        same = (tt >> shift) == (jj >> shift)
        e = [jnp.exp(e_all[ci][(2 + li) * c:(3 + li) * c]) for ci in chunks]
        qd = [jnp.where(second, q[ci] * e[ci], 0.0).astype(BF16) for ci in chunks]
        kd = [bd_qk(jnp.where(second, 0.0, k[ci] * e[ci])) for ci in chunks]
        scores = [scores[ci] + jnp.where(same, _dot_nt(qd[ci], kd[ci]), 0.0) for ci in chunks]

    v_bd = [jnp.concatenate([v_all[rw].astype(BF16)] * GLA_GROUP, axis=0) * mqv for rw in rows]
    o_intra = [_dot(scores[ci].astype(BF16), v_bd[ci]) for ci in chunks]
    upd = []
    for ci in chunks:
        ke = (k[ci] * jnp.exp(e_all[ci][c:2 * c])).astype(BF16)
        m = jnp.where(same_head_vq, _dot(v_all[rows[ci]].T.astype(BF16), ke), 0.0)
        dv = GLA_DV
        upd.append(m[0:dv] + m[dv:2 * dv] + m[2 * dv:3 * dv] + m[3 * dv:4 * dv])

    st = st_scr[...]
    os_ = []
    for ci in chunks:
        b = e_all[ci][0:c]
        st_bd = jnp.concatenate([st.astype(BF16)] * GLA_GROUP, axis=0) * mvq
        os_.append(_dot_nt((q[ci] * jnp.exp(b)).astype(BF16), st_bd) + o_intra[ci])
        st = st * jnp.exp(b[c - 1:c, :]) + upd[ci]
    st_scr[...] = st

    o = jnp.concatenate(os_, axis=0)
    g = g_ref[...].astype(F32)
    gs = g * _sigmoid(g) * nw_ref[...]
    outs = []
    for h in range(GLA_GROUP):
        oh = o[:, h * GLA_DV:(h + 1) * GLA_DV]
        oh = oh * lax.rsqrt(jnp.mean(oh * oh, axis=-1, keepdims=True) + RMS_EPS)
        outs.append(oh * gs[:, h * GLA_DV:(h + 1) * GLA_DV])
    o_ref[...] = jnp.concatenate(outs, axis=1).astype(o_ref.dtype)


def _gla_call(z, wup_pad, ba_row, nw_row, p_mat, masks, w_ff1, w_ff2, layer):
    t = z.shape[0]
    tb = CHUNK * GLA_NCH
    groups = GLA_HEADS // GLA_GROUP
    nsteps = t // tb
    total = groups * nsteps
    _, d, dff = w_ff1.shape
    assert d % total == 0 and dff % total == 0
    r1, r2 = d // total, dff // total
    blk = lambda w, off: pl.BlockSpec((tb, w), lambda p, c: (c, off // w + p))
    const = lambda a: pl.BlockSpec(a.shape, lambda p, c: (0, 0))
    return pl.pallas_call(
        _gla_kernel,
        out_shape=(jax.ShapeDtypeStruct((t, GLA_V), BF16), jax.ShapeDtypeStruct((d, dff), BF16),
                   jax.ShapeDtypeStruct((dff, d), BF16)),
        grid=(groups, nsteps),
        in_specs=[
            blk(GLA_QW, C_GQ), blk(GLA_QW, C_GK), blk(GLA_VW, C_GV), blk(GLA_VW, C_GG),
            pl.BlockSpec((tb, LANES), lambda p, c: (c, C_GA // LANES)),
            pl.BlockSpec((LANES, GLA_QW), lambda p, c: (0, p)),
            pl.BlockSpec((1, GLA_QW), lambda p, c: (0, p)),
            pl.BlockSpec((1, GLA_VW), lambda p, c: (0, p)),
            const(p_mat), const(masks[0]), const(masks[1]), const(masks[2]),
            pl.BlockSpec((None, r1, dff), lambda p, c: (layer, p * nsteps + c, 0)),
            pl.BlockSpec((None, r2, d), lambda p, c: (layer, p * nsteps + c, 0)),
        ],
        out_specs=(pl.BlockSpec((tb, GLA_VW), lambda p, c: (c, p)),
                   pl.BlockSpec((r1, dff), lambda p, c: (p * nsteps + c, 0)),
                   pl.BlockSpec((r2, d), lambda p, c: (p * nsteps + c, 0))),
        scratch_shapes=[pltpu.VMEM((GLA_DV, GLA_QW), F32)],
        compiler_params=pltpu.CompilerParams(
            dimension_semantics=("arbitrary", "arbitrary"), vmem_limit_bytes=VMEM_LIMIT_BYTES),
        name="gla_mixer",
    )(z, z, z, z, z, wup_pad, ba_row, nw_row, p_mat, *masks, w_ff1, w_ff2)


RWKV_NCH = 16
RWKV_GROUP = 4
RWKV_GW = RWKV_GROUP * RWKV_HEAD
RWKV_INV_LEVELS = (2, 4, 8, 16, 32)
RWKV_STATE_GROUP = 4


def _rwkv_kernel(*refs, has_vres):
    if has_vres:
        (r_ref, k_ref, v_ref, wl_ref, al_ref, gl_ref, vl_ref, vf_ref, vup_ref, v0_ref,
         w0_ref, wup_ref, a0_ref, aup_ref, gup_ref, kk_ref, ka_ref, rk_ref, gnw_ref, gnb_ref,
         tri_ref, ones_ref, wo_ref, o_ref, wob_ref, st_scr) = refs
    else:
        (r_ref, k_ref, v_ref, wl_ref, al_ref, gl_ref,
         w0_ref, wup_ref, a0_ref, aup_ref, gup_ref, kk_ref, ka_ref, rk_ref, gnw_ref, gnb_ref,
         tri_ref, ones_ref, wo_ref, o_ref, wob_ref, st_scr) = refs

    @pl.when(pl.program_id(1) == 0)
    def _():
        st_scr[...] = jnp.zeros(st_scr.shape, F32)

    wob_ref[...] = wo_ref[...].astype(BF16)

    c = CHUNK
    gw = RWKV_GW
    tt = lax.broadcasted_iota(jnp.int32, (c, gw), 0)
    jj = lax.broadcasted_iota(jnp.int32, (c, gw), 1) & (RWKV_HEAD - 1)
    strict = tt > jj
    incl = tt >= jj
    ones_bd = ones_ref[...]
    same_head_b = ones_bd != 0
    hshift = int(np.log2(RWKV_HEAD))
    same_head = ((lax.broadcasted_iota(jnp.int32, (gw, gw), 0) >> hshift)
                 == (lax.broadcasted_iota(jnp.int32, (gw, gw), 1) >> hshift))
    inv_n = 1.0 / RWKV_HEAD

    def seg_sum(x):
        return _dot_split2_rhs(x, ones_bd)

    def block_diag(x):
        return jnp.concatenate([x.astype(BF16)] * RWKV_GROUP, axis=0) * ones_bd

    tb = c * RWKV_NCH
    r_all = r_ref[...].astype(F32)
    k_all = k_ref[...].astype(F32)
    v_all = v_ref[...].astype(F32)
    w_pre = w0_ref[...] + _dot(jnp.tanh(wl_ref[...].astype(F32)).astype(BF16), wup_ref[...])
    lw_all = -jnp.exp(_log_sigmoid(w_pre) - 0.5)
    alr = _sigmoid(a0_ref[...] + _dot(al_ref[...].astype(BF16), aup_ref[...]))
    gate_all = _dot(_sigmoid(gl_ref[...].astype(F32)).astype(BF16), gup_ref[...])
    if has_vres:
        mix = _sigmoid(v0_ref[...] + _dot(vl_ref[...].astype(BF16), vup_ref[...]))
        v_all = v_all + (vf_ref[...].astype(F32) - v_all) * mix
    kk = k_all * kk_ref[...]
    k2_all = k_all * (1.0 + (alr - 1.0) * ka_ref[...])
    sums = seg_sum(jnp.concatenate([kk * kk, r_all * k2_all * rk_ref[...]], axis=0))
    kk = kk * lax.rsqrt(jnp.maximum(sums[:tb], 1e-24))
    bonus_all = sums[tb:] * v_all
    a_all = -kk
    b_all = kk * alr

    chunks = range(RWKV_NCH)
    rows = [slice(ci * c, (ci + 1) * c) for ci in chunks]
    cl = [_dot_split2_lhs(tri_ref[...], lw_all[rw]) for rw in rows]
    pre = []
    for ci in chunks:
        rw = rows[ci]
        cl_last = cl[ci][c - 1:c, :]
        e_neg = jnp.exp(-cl[ci])
        e_end = jnp.exp(cl_last - cl[ci])
        b, k2, v = b_all[rw], k2_all[rw], v_all[rw]
        at = a_all[rw] * jnp.exp(cl[ci] - lw_all[rw])
        rt = r_all[rw] * jnp.exp(cl[ci])
        pre.append(dict(
            rt=rt, v=v, at=at, g_c=jnp.exp(cl_last),
            lhs=jnp.concatenate([at, rt], axis=0).astype(BF16),
            rhs=jnp.concatenate([block_diag(b * e_neg), block_diag(k2 * e_neg)], axis=0),
            hat=jnp.concatenate([b * e_end, k2 * e_end], axis=0).astype(BF16)))
    for p in pre:
        aa = _dot_nt(p["lhs"], p["rhs"])
        p.update(
            a_ab=jnp.where(strict, aa[:c, :gw], 0.0),
            a_ak=jnp.where(strict, aa[:c, gw:], 0.0).astype(BF16),
            p_rb=jnp.where(incl, aa[c:, :gw], 0.0).astype(BF16),
            p_rk=jnp.where(incl, aa[c:, gw:], 0.0).astype(BF16))

    a_bd = [block_diag(p["a_ab"]) for p in pre]
    tinv = [jnp.where(tt == jj, 1.0, jnp.where((tt >> 1) == (jj >> 1), p["a_ab"], 0.0)) for p in pre]
    for s in RWKV_INV_LEVELS:
        shift = int(np.log2(2 * s))
        off = ((tt >> shift) == (jj >> shift)) & ((tt & s) != 0) & ((jj & s) == 0)
        t_bd = [block_diag(t) for t in tinv]
        half = [jnp.where(off, _dot(tinv[ci].astype(BF16), a_bd[ci]), 0.0).astype(BF16) for ci in chunks]
        tinv = [tinv[ci] + _dot(half[ci], t_bd[ci]) for ci in chunks]

    st = [st_scr[...]]
    ys = []

    def state_step(args):
        qh, y0, gmat, n0c, g_c = args
        ys.append(_dot_nt(qh, block_diag(st[0])) + y0)
        st[0] = st[0] * g_c + _dot(st[0].astype(BF16), gmat) + n0c

    pending = []
    for g0 in range(0, RWKV_NCH, RWKV_STATE_GROUP):
        grp = range(g0, min(g0 + RWKV_STATE_GROUP, RWKV_NCH))
        drain = iter(pending)
        t_b = {ci: tinv[ci].astype(BF16) for ci in grp}
        v_bd = {ci: block_diag(pre[ci]["v"]) for ci in grp}
        wt = {ci: _dot(t_b[ci], block_diag(pre[ci]["at"])) for ci in grp}
        akv = {ci: block_diag(_dot(pre[ci]["a_ak"], v_bd[ci])) for ci in grp}
        for args in drain:
            state_step(args)
            break
        u0 = {ci: _dot(t_b[ci], akv[ci]) for ci in grp}
        for args in drain:
            state_step(args)
            break
        qh = {ci: (pre[ci]["rt"] + _dot(pre[ci]["p_rb"], block_diag(wt[ci]))).astype(BF16) for ci in grp}
        gmat = {ci: jnp.where(same_head, _dot(wt[ci].T.astype(BF16), pre[ci]["hat"][:c]), 0.0).astype(BF16)
                for ci in grp}
        for args in drain:
            state_step(args)
            break
        y0 = {ci: _dot(pre[ci]["p_rb"], block_diag(u0[ci])) + _dot(pre[ci]["p_rk"], v_bd[ci]) for ci in grp}
        n0c = {}
        for ci in grp:
            uv_t = jnp.concatenate([u0[ci], pre[ci]["v"]], axis=0).T.astype(BF16)
            n0 = jnp.where(same_head, _dot(uv_t, pre[ci]["hat"]), 0.0)
            n0c[ci] = n0[0:c] + n0[c:2 * c] + n0[2 * c:3 * c] + n0[3 * c:4 * c]
        for args in drain:
            state_step(args)
        pending = [(qh[ci], y0[ci], gmat[ci], n0c[ci], pre[ci]["g_c"]) for ci in grp]
    for args in pending:
        state_step(args)
    st_scr[...] = st[0]

    y = jnp.concatenate(ys, axis=0)
    yc = y - seg_sum(y) * inv_n
    var = seg_sum(yc * yc) * inv_n
    yn = yc * lax.rsqrt(var + RWKV_GN_EPS) * gnw_ref[...] + gnb_ref[...]
    o_ref[...] = ((yn + bonus_all) * gate_all).astype(o_ref.dtype)


def _rwkv_call(z, z_first, vec, mats, tri, ones_bd, w_out, layer, has_vres):
    t = z.shape[0]
    tb = CHUNK * RWKV_NCH
    gw = RWKV_GW
    groups = RWKV_HEADS // RWKV_GROUP
    nsteps = t // tb
    _, dmix, d = w_out.shape
    assert dmix % (groups * nsteps) == 0
    rows_o = dmix // (groups * nsteps)
    col = lambda off: pl.BlockSpec((tb, gw), lambda p, c: (c, off // gw + p))
    fixed = lambda off, w: pl.BlockSpec((tb, w), lambda p, c: (c, off // w))
    rowv = pl.BlockSpec((1, gw), lambda p, c: (0, p))
    upm = lambda rows: pl.BlockSpec((rows, gw), lambda p, c: (0, p))
    const = lambda a: pl.BlockSpec(a.shape, lambda p, c: (0, 0))

    in_specs = [col(C_RR), col(C_RK), col(C_RV), fixed(C_RW, LANES), fixed(C_RA, LANES), fixed(C_RG, 2 * LANES)]
    args = [z, z, z, z, z, z]
    if has_vres:
        in_specs += [fixed(C_VRES, LANES), col(C_RV), upm(LANES), rowv]
        args += [z, z_first, mats["v_up"], vec["v0"]]
    in_specs += [rowv, upm(LANES), rowv, upm(LANES), upm(2 * LANES), rowv, rowv, rowv, rowv, rowv,
                 const(tri), const(ones_bd),
                 pl.BlockSpec((None, rows_o, d), lambda p, c: (layer, p * nsteps + c, 0))]
    args += [vec["w0"], mats["w_up"], vec["a0"], mats["a_up"], mats["g_up"], vec["k_k"], vec["k_a"],
             vec["r_k"], vec["gn_w"], vec["gn_b"], tri, ones_bd, w_out]
    return pl.pallas_call(
        functools.partial(_rwkv_kernel, has_vres=has_vres),
        out_shape=(jax.ShapeDtypeStruct((t, RWKV_WIDTH), BF16), jax.ShapeDtypeStruct((dmix, d), BF16)),
        grid=(groups, nsteps),
        in_specs=in_specs,
        out_specs=(pl.BlockSpec((tb, gw), lambda p, c: (c, p)),
                   pl.BlockSpec((rows_o, d), lambda p, c: (p * nsteps + c, 0))),
        scratch_shapes=[pltpu.VMEM((CHUNK, RWKV_GW), F32)],
        compiler_params=pltpu.CompilerParams(
            dimension_semantics=("arbitrary", "arbitrary"), vmem_limit_bytes=VMEM_LIMIT_BYTES),
        name="rwkv7_mixer",
    )(*args)


OUT_TM = 512
OUT_SUB = 128


def _outproj_kernel(x_ref, oa_ref, ob_ref, wa_ref, wb_ref, g_ref, gt_ref, gpre_ref, sh_ref, sc_ref, o_ref, h_ref):
    subs = [slice(s, s + OUT_SUB) for s in range(0, OUT_TM, OUT_SUB)]
    ys = [_dot(oa_ref[rw, :], wa_ref[...]) + _dot(ob_ref[rw, :], wb_ref[...]) for rw in subs]
    post_scale = gt_ref[0] * g_ref[...]
    pre_scale = gpre_ref[...] * (1.0 + sc_ref[0])
    for rw, y in zip(subs, ys):
        ms = jnp.mean(y * y, axis=-1, keepdims=True)
        xn = x_ref[rw, :] + (y * lax.rsqrt(ms + RMS_EPS)) * post_scale
        o_ref[rw, :] = xn
        ms2 = jnp.mean(xn * xn, axis=-1, keepdims=True)
        h_ref[rw, :] = ((xn * lax.rsqrt(ms2 + RMS_EPS)) * pre_scale + sh_ref[0]).astype(h_ref.dtype)


def _outproj_call(x2, o_gla, o_rwkv, w_out_all, g_row, gpre_row, mod, layer):
    t, d = x2.shape
    assert GLA_V == RWKV_WIDTH
    return pl.pallas_call(
        _outproj_kernel,
        out_shape=(jax.ShapeDtypeStruct((t, d), F32), jax.ShapeDtypeStruct((t, d), BF16)),
        grid=(t // OUT_TM,),
        in_specs=[
            pl.BlockSpec((OUT_TM, d), lambda i: (i, 0)),
            pl.BlockSpec((OUT_TM, GLA_V), lambda i: (i, 0)),
            pl.BlockSpec((OUT_TM, RWKV_WIDTH), lambda i: (i, 0)),
            pl.BlockSpec((GLA_V, d), lambda i: (0, 0)),
            pl.BlockSpec((RWKV_WIDTH, d), lambda i: (1, 0)),
            pl.BlockSpec((1, d), lambda i: (0, 0)),
            pl.BlockSpec((1, 1, d), lambda i: (layer, 0, 2)),
            pl.BlockSpec((1, d), lambda i: (0, 0)),
            pl.BlockSpec((1, 1, d), lambda i: (layer, 0, 3)),
            pl.BlockSpec((1, 1, d), lambda i: (layer, 0, 4)),
        ],
        out_specs=(pl.BlockSpec((OUT_TM, d), lambda i: (i, 0)), pl.BlockSpec((OUT_TM, d), lambda i: (i, 0))),
        compiler_params=pltpu.CompilerParams(
            dimension_semantics=("arbitrary",), vmem_limit_bytes=VMEM_LIMIT_BYTES),
        name="outproj",
    )(x2, o_gla, o_rwkv, w_out_all, w_out_all, g_row, mod, gpre_row, mod, mod)


FFN_TM = 512
FFN_TF = 1024
FFN_SUB = 256


def _ffn_kernel(x_ref, h_ref, w1_ref, w2_ref, gpost_ref, gt_ref, o_ref, acc_scr):
    f = pl.program_id(1)
    last = pl.num_programs(1) - 1

    def partial(rows):
        u = jnp.maximum(_dot(h_ref[rows, :], w1_ref[...]), 0.0)
        return _dot((u * u).astype(BF16), w2_ref[...])

    everything = slice(0, FFN_TM)

    @pl.when(f == 0)
    def _():
        acc_scr[...] = partial(everything)

    @pl.when((f > 0) & (f < last))
    def _():
        acc_scr[...] += partial(everything)

    @pl.when(f == last)
    def _():
        post_scale = gt_ref[0] * gpost_ref[...]
        for s in range(0, FFN_TM, FFN_SUB):
            rows = slice(s, s + FFN_SUB)
            y = acc_scr[rows, :] + partial(rows)
            ms = jnp.mean(y * y, axis=-1, keepdims=True)
            o_ref[rows, :] = x_ref[rows, :] + (y * lax.rsqrt(ms + RMS_EPS)) * post_scale


def _ffn_call(x2, h2, gpost_row, mod, layer, w1, w2):
    t, d = x2.shape
    dff = w1.shape[1]
    return pl.pallas_call(
        _ffn_kernel,
        out_shape=jax.ShapeDtypeStruct((t, d), F32),
        grid=(t // FFN_TM, dff // FFN_TF),
        in_specs=[
            pl.BlockSpec((FFN_TM, d), lambda i, f: (i, 0)),
            pl.BlockSpec((FFN_TM, d), lambda i, f: (i, 0)),
            pl.BlockSpec((d, FFN_TF), lambda i, f: (0, f)),
            pl.BlockSpec((FFN_TF, d), lambda i, f: (f, 0)),
            pl.BlockSpec((1, d), lambda i, f: (0, 0)),
            pl.BlockSpec((1, 1, d), lambda i, f: (layer, 0, 5)),
        ],
        out_specs=pl.BlockSpec((FFN_TM, d), lambda i, f: (i, 0)),
        scratch_shapes=[pltpu.VMEM((FFN_TM, d), F32)],
        compiler_params=pltpu.CompilerParams(
            dimension_semantics=("arbitrary", "arbitrary"), vmem_limit_bytes=VMEM_LIMIT_BYTES),
        name="ffn",
    )(x2, h2, w1, w2, gpost_row, mod)


def _pad_cols(w, width):
    return jnp.pad(w, ((0, 0), (0, width - w.shape[1])))


def _pad_rows(w, rows):
    return jnp.pad(w, ((0, rows - w.shape[0]), (0, 0)))


def _pack_inproj(w_in_t, rwkv_mu, vres_w_down, vres_mu):
    n_layers, _, d = w_in_t.shape
    gla_cols = N_MAIN + GLA_GATE_RANK
    rkv_end = gla_cols + N_MAIN
    w_end, a_end = rkv_end + RWKV_W_RANK, rkv_end + RWKV_W_RANK + RWKV_A_RANK
    pad_to = lambda w, n: jnp.pad(w, ((0, 0), (0, n - w.shape[1]), (0, 0)))
    vres_t = jnp.concatenate(
        [jnp.zeros((1, RWKV_V_RANK, d), BF16), jnp.swapaxes(vres_w_down, 1, 2).astype(BF16)], axis=0)
    w_run2_t = w_in_t[:, gla_cols:rkv_end]
    w_extra_t = jnp.concatenate(
        [pad_to(w_in_t[:, N_MAIN:gla_cols], LANES), pad_to(vres_t, LANES), w_in_t[:, a_end:],
         pad_to(w_in_t[:, rkv_end:w_end], LANES), pad_to(w_in_t[:, w_end:a_end], LANES)], axis=1)
    vmu = jnp.concatenate([jnp.zeros((1, RWKV_V_RANK), F32), vres_mu], axis=0)
    padc = lambda m, n: jnp.pad(m, ((0, 0), (0, n - m.shape[1])))
    mu_pack = jnp.concatenate(
        [jnp.zeros((n_layers, N_MAIN), F32), rwkv_mu[:, :N_MAIN], jnp.zeros((n_layers, LANES), F32), padc(vmu, LANES),
         rwkv_mu[:, N_MAIN + RWKV_W_RANK + RWKV_A_RANK:], padc(rwkv_mu[:, N_MAIN:N_MAIN + RWKV_W_RANK], LANES),
         padc(rwkv_mu[:, N_MAIN + RWKV_W_RANK:N_MAIN + RWKV_W_RANK + RWKV_A_RANK], LANES)], axis=1)
    return w_run2_t, w_extra_t, mu_pack[:, None, :]


def kernel(x, c, w_ada, b_ada, g_pre_mix, g_post_mix, g_pre_ffn, g_post_ffn, w_in, gla_w_a_up, gla_b_a, gla_norm_w, rwkv_mu, rwkv_w0, rwkv_w_up, rwkv_a0, rwkv_a_up, rwkv_g_up, rwkv_k_k, rwkv_k_a, rwkv_r_k, rwkv_gn_w, rwkv_gn_b, vres_w_down, vres_mu, vres_up, vres_v0, w_out, w_ff1, w_ff2):
    bsz, t, d = x.shape
    assert bsz == 1 and d == D_MODEL and t % (CHUNK * max(GLA_NCH, RWKV_NCH)) == 0
    assert t % FFN_TM == 0 and t % INP_TM == 0 and t % OUT_TM == 0
    n_layers = w_ada.shape[0]

    mod = _ada_call(c.reshape(d, 1), w_ada, b_ada.reshape(n_layers, 1, 6 * d))

    p_gla = jnp.asarray(_gla_exponent_matrix(), BF16)
    gla_masks = tuple(jnp.asarray(m, BF16) for m in _gla_masks())
    tri = jnp.asarray(np.tril(np.ones((CHUNK, CHUNK), np.float32)), BF16)
    head_of_lane = np.arange(RWKV_GW) // RWKV_HEAD
    ones_bd = jnp.asarray((head_of_lane[:, None] == head_of_lane[None, :]).astype(np.float32), BF16)

    w_in_t = jnp.swapaxes(w_in, 1, 2).astype(BF16)
    w_run2_t, w_extra_t, mu_pack = _pack_inproj(w_in_t, rwkv_mu, vres_w_down, vres_mu)
    x2 = x.reshape(t, d)
    z_first = None
    for i in range(n_layers):
        j = i - 1
        z = _inproj_call(x2, g_pre_mix[i][None, :], mod, i, w_in_t, w_run2_t, w_extra_t, mu_pack[i])
        if i == 0:
            z_first = z

        o_gla, w_ff1_b, w_ff2_b = _gla_call(
            z, _pad_rows(gla_w_a_up[i], LANES).astype(BF16), gla_b_a[i][None, :], gla_norm_w[i][None, :], p_gla,
            gla_masks, w_ff1, w_ff2, i)

        row = lambda a: a.reshape(1, RWKV_WIDTH)
        vec = dict(w0=row(rwkv_w0[i]), a0=row(rwkv_a0[i]), k_k=row(rwkv_k_k[i]), k_a=row(rwkv_k_a[i]),
                   r_k=row(rwkv_r_k[i]), gn_w=row(rwkv_gn_w[i]), gn_b=row(rwkv_gn_b[i]))
        mats = dict(w_up=_pad_rows(rwkv_w_up[i], LANES).astype(BF16),
                    a_up=_pad_rows(rwkv_a_up[i], LANES).astype(BF16),
                    g_up=rwkv_g_up[i].astype(BF16))
        if i > 0:
            vec["v0"] = row(vres_v0[j])
            mats["v_up"] = _pad_rows(vres_up[j], LANES).astype(BF16)
        o_rwkv, w_out_b = _rwkv_call(z, z_first, vec, mats, tri, ones_bd, w_out, i, has_vres=i > 0)

        x2, h2 = _outproj_call(x2, o_gla, o_rwkv, w_out_b, g_post_mix[i][None, :], g_pre_ffn[i][None, :], mod, i)
        x2 = _ffn_call(x2, h2, g_post_ffn[i][None, :], mod, i, w_ff1_b, w_ff2_b)
    return x2.reshape(bsz, t, d)
```
